```python
import math
import jax
import jax.numpy as jnp
from jax import lax
import numpy as np

D_MODEL = 1024
BATCH = 32
SEQ = 2048
DEPTH = 2

CHUNK = 64
RMS_EPS = 1e-6

CONV_WIDTH = D_MODEL // 2
SHORT_CONV_K = 3
SB_HEADS = 8
SB_HEAD_DIM = 64
SB_WIDTH = SB_HEADS * SB_HEAD_DIM
Q_BLOCK = 128
EVEN_PROJ = 3 * CONV_WIDTH + 3 * SB_WIDTH
EVEN_MIX = CONV_WIDTH + SB_WIDTH

D_INNER = 2 * D_MODEL
SSM_HEAD_DIM = 64
SSM_HEADS = D_INNER // SSM_HEAD_DIM
SSM_GROUPS = 8
SSM_HPG = SSM_HEADS // SSM_GROUPS
SSM_STATE = 128
SSM_CONV_K = 4
SSM_CONV_DIM = D_INNER + 2 * SSM_GROUPS * SSM_STATE
ODD_PROJ = D_INNER + SSM_CONV_DIM + SSM_HEADS

N_EXPERT_GROUPS = 4
EXPERTS_PER_GROUP = 8
N_EXPERTS = N_EXPERT_GROUPS * EXPERTS_PER_GROUP
TOP_K_EXPERT = 2
EXPERT_FF = 512
MOE_BLOCK = 256

kernel_name = 'hybrid_conv_stickbreak_ssd_hmoe'


def rmsnorm(x, w):
    xf = x.astype(jnp.float32)
    y = xf * lax.rsqrt(jnp.mean(xf * xf, axis=-1, keepdims=True) + RMS_EPS)
    return (y * w.astype(jnp.float32)).astype(x.dtype)


def causal_dwconv(x, w):
    k, c = w.shape
    return lax.conv_general_dilated(
        x, w[:, None, :].astype(x.dtype), window_strides=(1,), padding=[(k - 1, 0)],
        dimension_numbers=('NWC', 'WIO', 'NWC'), feature_group_count=c)


def stick_breaking_attention(q, k, v):
    seq, dh = q.shape[1], q.shape[3]
    qf = q.astype(jnp.float32) * (dh ** -0.5)
    kf = k.astype(jnp.float32)
    vf = v.astype(jnp.float32)
    outs = []
    for blk in range(seq // Q_BLOCK):
        q0 = blk * Q_BLOCK
        q_end = q0 + Q_BLOCK
        z = jnp.einsum('bqhd,bkhd->bhqk', qf[:, q0:q_end], kf[:, :q_end])
        t_idx = q0 + jnp.arange(Q_BLOCK)
        s_idx = jnp.arange(q_end)
        causal = s_idx[None, :] < t_idx[:, None]
        log_1m = jnp.where(causal, -jax.nn.softplus(z), 0.0)
        suffix = lax.cumsum(log_1m, axis=3, reverse=True) - log_1m
        att = jnp.where(causal, jnp.exp(jax.nn.log_sigmoid(z) + suffix), 0.0)
        outs.append(jnp.einsum('bhqk,bkhd->bqhd', att, vf[:, :q_end]))
    return jnp.concatenate(outs, axis=1)


def conv_stick_mixer(h, w_in, conv_w, w_out):
    bsz, seq, _ = h.shape
    proj = h @ w_in
    c0 = CONV_WIDTH
    gate_b = proj[..., :c0]
    gate_c = proj[..., c0:2 * c0]
    hid = proj[..., 2 * c0:3 * c0]
    off = 3 * c0
    q = proj[..., off:off + SB_WIDTH].reshape(bsz, seq, SB_HEADS, SB_HEAD_DIM)
    k = proj[..., off + SB_WIDTH:off + 2 * SB_WIDTH].reshape(bsz, seq, SB_HEADS, SB_HEAD_DIM)
    v = proj[..., off + 2 * SB_WIDTH:off + 3 * SB_WIDTH].reshape(bsz, seq, SB_HEADS, SB_HEAD_DIM)
    conv_out = gate_b * causal_dwconv(gate_c * hid, conv_w)
    sb_out = stick_breaking_attention(q, k, v).reshape(bsz, seq, SB_WIDTH).astype(h.dtype)
    return jnp.concatenate([conv_out, sb_out], axis=-1) @ w_out


def ssd_scan(x, dt, a, b_mat, c_mat):
    bsz, seq = x.shape[0], x.shape[1]
    n_chunks = seq // CHUNK

    def to_chunks(t):
        return jnp.moveaxis(t.reshape((bsz, n_chunks, CHUNK) + t.shape[2:]), 1, 0)

    lower = jnp.tril(jnp.ones((CHUNK, CHUNK), dtype=bool))

    def step(state, inp):
        xc, dtc, bc, cc = inp
        cs = jnp.cumsum(dtc * a, axis=1)
        seg = cs[:, :, None] - cs[:, None, :]
        decay = jnp.exp(jnp.where(lower[None, :, :, None, None], seg, -jnp.inf))
        xdt = xc * dtc[..., None]
        cb = jnp.einsum('blgn,bsgn->blsg', cc, bc)
        y = jnp.einsum('blsgr,bsgrp->blgrp', cb[..., None] * decay, xdt)
        y = y + jnp.einsum('blgn,bgrpn->blgrp', cc, state) * jnp.exp(cs)[..., None]
        to_end = jnp.exp(cs[:, -1:] - cs)
        state = (state * jnp.exp(cs[:, -1])[..., None, None]
                 + jnp.einsum('bsgn,bsgrp->bgrpn', bc, xdt * to_end[..., None]))
        return state, y

    state0 = jnp.zeros((bsz, SSM_GROUPS, SSM_HPG, SSM_HEAD_DIM, SSM_STATE), jnp.float32)
    _, ys = lax.scan(step, state0, (to_chunks(x), to_chunks(dt), to_chunks(b_mat), to_chunks(c_mat)))
    return jnp.moveaxis(ys, 0, 1).reshape(x.shape)


def mamba2_mixer(h, in_proj, conv_w, conv_b, dt_bias, a_log, d_skip, norm_w, out_proj):
    bsz, seq, _ = h.shape
    proj = h @ in_proj
    z = proj[..., :D_INNER]
    xbc = proj[..., D_INNER:D_INNER + SSM_CONV_DIM]
    dt_raw = proj[..., D_INNER + SSM_CONV_DIM:]
    xbc = jax.nn.silu(causal_dwconv(xbc, conv_w) + conv_b)
    gn = SSM_GROUPS * SSM_STATE
    xs = xbc[..., :D_INNER].astype(jnp.float32).reshape(bsz, seq, SSM_GROUPS, SSM_HPG, SSM_HEAD_DIM)
    b_mat = xbc[..., D_INNER:D_INNER + gn].astype(jnp.float32).reshape(bsz, seq, SSM_GROUPS, SSM_STATE)
    c_mat = xbc[..., D_INNER + gn:].astype(jnp.float32).reshape(bsz, seq, SSM_GROUPS, SSM_STATE)
    dt = jax.nn.softplus(dt_raw.astype(jnp.float32) + dt_bias.astype(jnp.float32))
    dt = dt.reshape(bsz, seq, SSM_GROUPS, SSM_HPG)
    a = -jnp.exp(a_log.astype(jnp.float32)).reshape(SSM_GROUPS, SSM_HPG)
    y = ssd_scan(xs, dt, a, b_mat, c_mat)
    y = y + d_skip.astype(jnp.float32).reshape(SSM_GROUPS, SSM_HPG)[:, :, None] * xs
    y = y.reshape(bsz, seq, D_INNER) * jax.nn.silu(z.astype(jnp.float32))
    yg = y.reshape(bsz, seq, SSM_GROUPS, D_INNER // SSM_GROUPS)
    yg = yg * lax.rsqrt(jnp.mean(yg * yg, axis=-1, keepdims=True) + RMS_EPS)
    y = yg.reshape(bsz, seq, D_INNER) * norm_w.astype(jnp.float32)
    return y.astype(h.dtype) @ out_proj


def hierarchical_moe(x, wg, bg, we, be, w_gate, w_up, w_down):
    bsz, seq, d = x.shape
    xt = x.reshape(-1, d)
    n_tok = xt.shape[0]
    g_prob = jax.nn.softmax((xt @ wg + bg).astype(jnp.float32), axis=-1)
    g_top_p, g_idx = lax.top_k(g_prob, 1)
    e_logits = jnp.einsum('td,gde->tge', xt, we) + be
    e_logits = e_logits[jnp.arange(n_tok), g_idx[:, 0]].astype(jnp.float32)
    e_top_l, e_idx = lax.top_k(e_logits, TOP_K_EXPERT)
    gates = jax.nn.softmax(e_top_l, axis=-1) * g_top_p
    expert = g_idx * EXPERTS_PER_GROUP + e_idx
    n_slots = n_tok * TOP_K_EXPERT
    flat = expert.reshape(-1)
    order = jnp.argsort(flat)
    sorted_e = flat[order]
    counts = jnp.bincount(flat, length=N_EXPERTS)
    starts = jnp.cumsum(counts) - counts
    padded = (counts + MOE_BLOCK - 1) // MOE_BLOCK * MOE_BLOCK
    pends = jnp.cumsum(padded)
    pstarts = pends - padded
    dest = pstarts[sorted_e] + jnp.arange(n_slots) - starts[sorted_e]
    n_blocks = -(-n_slots // MOE_BLOCK) + N_EXPERTS
    tok = order // TOP_K_EXPERT
    buf = jnp.zeros((n_blocks * MOE_BLOCK, d), x.dtype).at[dest].set(xt[tok])
    block_e = jnp.minimum(jnp.searchsorted(pends, jnp.arange(n_blocks) * MOE_BLOCK, side='right'),
                          N_EXPERTS - 1)

    def expert_block(args):
        xb, e = args
        hb = jax.nn.silu(xb @ w_gate[e]) * (xb @ w_up[e])
        return hb @ w_down[e]

    out = lax.map(expert_block, (buf.reshape(n_blocks, MOE_BLOCK, d), block_e)).reshape(-1, d)
    w_sorted = gates.reshape(-1)[order].astype(x.dtype)
    y = jax.ops.segment_sum(out[dest] * w_sorted[:, None], tok, num_segments=n_tok)
    return y.reshape(bsz, seq, d)


def setup_inputs(seed: int = 0) -> dict:
    key = jax.random.key(seed)
    ks = jax.random.split(key, 24)
    n_even = (DEPTH + 1) // 2
    n_odd = DEPTH // 2

    def normal(k, shape, scale):
        return jax.random.normal(k, shape, jnp.float32) * scale

    x = normal(ks[0], (BATCH, SEQ, D_MODEL), 1.0)
    even_w_in = normal(ks[1], (n_even, D_MODEL, EVEN_PROJ), D_MODEL ** -0.5)
    even_conv_w = normal(ks[2], (n_even, SHORT_CONV_K, CONV_WIDTH), SHORT_CONV_K ** -0.5)
    even_w_out = normal(ks[3], (n_even, EVEN_MIX, D_MODEL), EVEN_MIX ** -0.5)
    odd_in_proj = normal(ks[4], (n_odd, D_MODEL, ODD_PROJ), D_MODEL ** -0.5)
    odd_conv_w = normal(ks[5], (n_odd, SSM_CONV_K, SSM_CONV_DIM), SSM_CONV_K ** -0.5)
    odd_conv_b = normal(ks[6], (n_odd, SSM_CONV_DIM), 0.02)
    dt0 = jnp.exp(jax.random.uniform(ks[7], (n_odd, SSM_HEADS), jnp.float32,
                                     minval=math.log(1e-3), maxval=math.log(1e-1)))
    odd_dt_bias = dt0 + jnp.log(-jnp.expm1(-dt0))
    odd_a_log = jnp.log(jax.random.uniform(ks[8], (n_odd, SSM_HEADS), jnp.float32, minval=1.0, maxval=16.0))
    odd_d_skip = 1.0 + normal(ks[9], (n_odd, SSM_HEADS), 0.1)
    odd_norm_w = 1.0 + normal(ks[10], (n_odd, D_INNER), 0.02)
    odd_out_proj = normal(ks[11], (n_odd, D_INNER, D_MODEL), D_INNER ** -0.5)
    norm_mix = 1.0 + normal(ks[12], (DEPTH, D_MODEL), 0.02)
    norm_ffn = 1.0 + normal(ks[13], (DEPTH, D_MODEL), 0.02)
    router_g_w = normal(ks[14], (DEPTH, D_MODEL, N_EXPERT_GROUPS), D_MODEL ** -0.5)
    router_g_b = normal(ks[15], (DEPTH, N_EXPERT_GROUPS), 0.01)
    router_e_w = normal(ks[16], (DEPTH, N_EXPERT_GROUPS, D_MODEL, EXPERTS_PER_GROUP), D_MODEL ** -0.5)
    router_e_b = normal(ks[17], (DEPTH, N_EXPERT_GROUPS, EXPERTS_PER_GROUP), 0.01)
    expert_w_gate = normal(ks[18], (DEPTH, N_EXPERTS, D_MODEL, EXPERT_FF), D_MODEL ** -0.5)
    expert_w_up = normal(ks[19], (DEPTH, N_EXPERTS, D_MODEL, EXPERT_FF), D_MODEL ** -0.5)
    expert_w_down = normal(ks[20], (DEPTH, N_EXPERTS, EXPERT_FF, D_MODEL), EXPERT_FF ** -0.5)
    final_norm = 1.0 + normal(ks[21], (D_MODEL,), 0.02)
    return {'x': x, 'even_w_in': even_w_in, 'even_conv_w': even_conv_w, 'even_w_out': even_w_out,
            'odd_in_proj': odd_in_proj, 'odd_conv_w': odd_conv_w, 'odd_conv_b': odd_conv_b,
            'odd_dt_bias': odd_dt_bias, 'odd_a_log': odd_a_log, 'odd_d_skip': odd_d_skip,
            'odd_norm_w': odd_norm_w, 'odd_out_proj': odd_out_proj,
            'norm_mix': norm_mix, 'norm_ffn': norm_ffn,
            'router_g_w': router_g_w, 'router_g_b': router_g_b,
            'router_e_w': router_e_w, 'router_e_b': router_e_b,
            'expert_w_gate': expert_w_gate, 'expert_w_up': expert_w_up, 'expert_w_down': expert_w_down,
            'final_norm': final_norm}


def reference(x, even_w_in, even_conv_w, even_w_out, odd_in_proj, odd_conv_w, odd_conv_b,
              odd_dt_bias, odd_a_log, odd_d_skip, odd_norm_w, odd_out_proj, norm_mix, norm_ffn,
              router_g_w, router_g_b, router_e_w, router_e_b, expert_w_gate, expert_w_up,
              expert_w_down, final_norm):
    h = x
    for layer in range(DEPTH):
        i = layer // 2
        hn = rmsnorm(h, norm_mix[layer])
        if layer % 2 == 0:
            mix = conv_stick_mixer(hn, even_w_in[i], even_conv_w[i], even_w_out[i])
        else:
            mix = mamba2_mixer(hn, odd_in_proj[i], odd_conv_w[i], odd_conv_b[i], odd_dt_bias[i],
                               odd_a_log[i], odd_d_skip[i], odd_norm_w[i], odd_out_proj[i])
        h = h + mix
        h = h + hierarchical_moe(rmsnorm(h, norm_ffn[layer]), router_g_w[layer], router_g_b[layer],
                                 router_e_w[layer], router_e_b[layer], expert_w_gate[layer],
                                 expert_w_up[layer], expert_w_down[layer])
    return rmsnorm(h, final_norm)
```

```python
import functools

import numpy as np
import jax
import jax.numpy as jnp
from jax import lax
from jax.experimental import pallas as pl
from jax.experimental.pallas import tpu as pltpu

F32 = jnp.float32
BF16 = jnp.bfloat16

D_MODEL = 1024
RMS_EPS = 1e-6
CONV_WIDTH = 512
SHORT_CONV_K = 3
SB_HEAD_DIM = 64
SB_WIDTH = 512
D_INNER = 2048
SSM_HEAD_DIM = 64
SSM_HEADS = 32
SSM_GROUPS = 8
SSM_HPG = 4
SSM_STATE = 128
SSM_CONV_K = 4
SSM_GN = SSM_GROUPS * SSM_STATE
SSM_CONV_DIM = D_INNER + 2 * SSM_GN
N_EXPERT_GROUPS = 4
EXPERTS_PER_GROUP = 8
EXPERT_FF = 512
N_PAIRS = EXPERTS_PER_GROUP * (EXPERTS_PER_GROUP - 1) // 2
N_BUCKETS = N_EXPERT_GROUPS * N_PAIRS

LANES = 128
SUBLANES = 8
VMEM_LIMIT = 56 * 1024 * 1024

ROW_TILE = 512
SB_BLOCK = 256
SSD_CHUNK = 128
MOE_BLOCK = 256


def _params(*sem):
    return pltpu.CompilerParams(dimension_semantics=sem, vmem_limit_bytes=VMEM_LIMIT)


def _rms(x, w):
    return x * lax.rsqrt(jnp.mean(x * x, axis=-1, keepdims=True) + RMS_EPS) * w


def _dot(a, b):
    return jnp.dot(a, b, preferred_element_type=F32)


def _dot_nt(a, b):
    return lax.dot_general(a, b, (((1,), (1,)), ((), ())), preferred_element_type=F32)


def _even_in_kernel(h_ref, nw_ref, w_ref, cw_ref, conv_ref, q_ref, k_ref, v_ref, ubuf,
                    *, tm, tiles_per_seq):
    i = pl.program_id(0)
    xb = _rms(h_ref[...], nw_ref[...]).astype(BF16)
    c = CONV_WIDTH

    def proj(c0):
        return _dot(xb, w_ref[:, c0:c0 + c])

    gate_b = proj(0)
    u = proj(c) * proj(2 * c)

    @pl.when(i % tiles_per_seq == 0)
    def _():
        ubuf[0:SUBLANES, :] = jnp.zeros((SUBLANES, c), F32)

    ubuf[SUBLANES:SUBLANES + tm, :] = u
    cw = cw_ref[...]
    conv = cw[2:3, :] * u + cw[1:2, :] * ubuf[SUBLANES - 1:SUBLANES - 1 + tm, :] \
        + cw[0:1, :] * ubuf[SUBLANES - 2:SUBLANES - 2 + tm, :]
    conv_ref[...] = (gate_b * conv).astype(BF16)
    ubuf[0:SUBLANES, :] = ubuf[tm:tm + SUBLANES, :]
    q_ref[...] = (proj(3 * c) * (SB_HEAD_DIM ** -0.5)).astype(BF16)
    k_ref[...] = proj(4 * c).astype(BF16)
    v_ref[...] = proj(5 * c).astype(BF16)


def _even_in_proj(h, n_tok, seq, nw, w_in, conv_w):
    tm = ROW_TILE
    c = CONV_WIDTH
    row = lambda i: (i, 0)
    fixed = lambda i: (0, 0)
    out = jax.ShapeDtypeStruct((n_tok, c), BF16)
    return pl.pallas_call(
        functools.partial(_even_in_kernel, tm=tm, tiles_per_seq=seq // tm),
        grid=(n_tok // tm,),
        in_specs=[pl.BlockSpec((tm, D_MODEL), row), pl.BlockSpec((1, D_MODEL), fixed),
                  pl.BlockSpec((D_MODEL, 6 * c), fixed), pl.BlockSpec((SHORT_CONV_K, c), fixed)],
        out_specs=[pl.BlockSpec((tm, c), row)] * 4,
        out_shape=[out] * 4,
        scratch_shapes=[pltpu.VMEM((tm + SUBLANES, c), F32)],
        compiler_params=_params("arbitrary"),
        name="even_in_proj",
    )(h, nw, w_in, conv_w)


def _sb_kernel(q_ref, k_ref, v_ref, tri_ref, o_ref, *, seq, bq):
    nq = seq // bq
    lane = lax.broadcasted_iota(jnp.int32, (bq, LANES), 1)
    rr = lax.broadcasted_iota(jnp.int32, (bq, bq), 0)
    cc = lax.broadcasted_iota(jnp.int32, (bq, bq), 1)
    causal = cc < rr
    tri = tri_ref[...]

    def block(qm, kj, carry, masked):
        run, acc = carry
        kb = k_ref[pl.ds(pl.multiple_of(kj * bq, bq), bq), :]
        vb = v_ref[pl.ds(pl.multiple_of(kj * bq, bq), bq), :]
        z = _dot_nt(qm, kb)
        sp = jnp.maximum(z, 0.0) + jnp.log(1.0 + jnp.exp(-jnp.abs(z)))
        log_1m = -sp
        if masked:
            log_1m = jnp.where(causal, log_1m, 0.0)
        suffix = _dot(log_1m.astype(BF16), tri) + run
        p = jnp.exp((z - sp) + suffix)
        if masked:
            p = jnp.where(causal, p, 0.0)
        acc = acc + _dot(p.astype(BF16), vb)
        run = run + jnp.sum(log_1m, axis=1, keepdims=True)
        return run, acc

    for qi in range(nq):
        q = q_ref[qi * bq:(qi + 1) * bq, :]
        outs = []
        for head in range(2):
            qm = jnp.where((lane < SB_HEAD_DIM) == (head == 0), q, jnp.zeros_like(q))
            carry = (jnp.zeros((bq, 1), F32), jnp.zeros((bq, LANES), F32))
            carry = block(qm, qi, carry, True)
            if qi > 0:
                carry = lax.fori_loop(
                    0, qi, lambda it, cr, qm=qm, qi=qi: block(qm, qi - 1 - it, cr, False), carry)
            outs.append(carry[1])
        o_ref[qi * bq:(qi + 1) * bq, :] = jnp.where(lane < SB_HEAD_DIM, outs[0], outs[1]).astype(BF16)


def _stick_breaking(q, k, v, n_tok, seq):
    bq = SB_BLOCK
    tri = jnp.asarray(np.tril(np.ones((bq, bq), np.float32), -1), BF16)
    blk = pl.BlockSpec((seq, LANES), lambda b, hp: (b, hp))
    return pl.pallas_call(
        functools.partial(_sb_kernel, seq=seq, bq=bq),
        grid=(n_tok // seq, SB_WIDTH // LANES),
        in_specs=[blk, blk, blk, pl.BlockSpec((bq, bq), lambda b, hp: (0, 0))],
        out_specs=blk,
        out_shape=jax.ShapeDtypeStruct((n_tok, SB_WIDTH), BF16),
        compiler_params=_params("arbitrary", "arbitrary"),
        name="stick_breaking",
    )(q, k, v, tri)


def _mix_out_kernel(*refs, n_act, tm):
    h_ref = refs[0]
    act_refs = refs[1:1 + n_act]
    w_refs = refs[1 + n_act:1 + 2 * n_act]
    nw_ref, wrh_ref, wrl_ref, br_ref, triu_ref = refs[1 + 2 * n_act:6 + 2 * n_act]
    hout_ref, ri_ref, rf_ref, cnt_ref, carry = refs[6 + 2 * n_act:]
    i = pl.program_id(0)

    @pl.when(i == 0)
    def _():
        carry[...] = jnp.zeros(carry.shape, F32)

    acc = h_ref[...]
    for a_ref, w_ref in zip(act_refs, w_refs):
        acc = acc + _dot(a_ref[...], w_ref[...])
    hout_ref[...] = acc

    xn = _rms(acc, nw_ref[...])
    xh = xn.astype(BF16)
    xl = (xn - xh.astype(F32)).astype(BF16)
    wrh = wrh_ref[...]
    logits = _dot(xh, wrh) + _dot(xl, wrh) + _dot(xh, wrl_ref[...]) + br_ref[...]
    lt = logits.T
    r = lax.broadcasted_iota(jnp.int32, (LANES, tm), 0).astype(F32)
    big = float(LANES)
    neg = -jnp.inf

    def first_argmax(vals):
        m = jnp.max(vals, axis=0, keepdims=True)
        return m, jnp.min(jnp.where(vals == m, r, big), axis=0, keepdims=True)

    gl = jnp.where(r < N_EXPERT_GROUPS, lt, neg)
    gmax, gidx = first_argmax(gl)
    g_top_p = 1.0 / jnp.sum(jnp.exp(gl - gmax), axis=0, keepdims=True)
    lo = N_EXPERT_GROUPS + EXPERTS_PER_GROUP * gidx
    el = jnp.where((r >= lo) & (r < lo + EXPERTS_PER_GROUP), lt, neg)
    m1, i1 = first_argmax(el)
    m2, i2 = first_argmax(jnp.where(r == i1, neg, el))
    t = jnp.exp(m2 - m1)
    g1 = g_top_p / (1.0 + t)
    g2 = g_top_p * t / (1.0 + t)
    e1 = i1 - lo
    e2 = i2 - lo
    ea = jnp.minimum(e1, e2)
    eb = jnp.maximum(e1, e2)
    pair = ea * (2 * EXPERTS_PER_GROUP - 1 - ea) * 0.5 + (eb - ea - 1.0)
    bucket = gidx * N_PAIRS + pair
    first_is_a = e1 < e2
    gate_a = jnp.where(first_is_a, g1, g2)
    gate_b = jnp.where(first_is_a, g2, g1)

    onehot = r == bucket
    cum = _dot(jnp.where(onehot, 1.0, 0.0).astype(BF16), triu_ref[...])
    prev = carry[...]
    rank = jnp.sum(jnp.where(onehot, cum + prev[:, 0:1], 0.0), axis=0, keepdims=True)
    new = prev + jnp.sum(jnp.where(onehot, 1.0, 0.0), axis=1, keepdims=True)
    carry[...] = new
    cnt_ref[...] = new

    row8 = lax.broadcasted_iota(jnp.int32, (SUBLANES, tm), 0)
    ri_ref[...] = jnp.where(row8 == 0, bucket, jnp.where(row8 == 1, rank, 0.0)).astype(jnp.int32)
    rf_ref[...] = jnp.where(row8 == 0, gate_a, jnp.where(row8 == 1, gate_b, 0.0))


def _mix_out_router(h, acts, ws, n_tok, nw, wr_hi, wr_lo, br):
    tm = ROW_TILE
    row = lambda i: (i, 0)
    col = lambda i: (0, i)
    fixed = lambda i: (0, 0)
    triu = jnp.asarray(np.triu(np.ones((tm, tm), np.float32), 1), BF16)
    in_specs = [pl.BlockSpec((tm, D_MODEL), row)]
    in_specs += [pl.BlockSpec((tm, a.shape[1]), row) for a in acts]
    in_specs += [pl.BlockSpec(w.shape, fixed) for w in ws]
    in_specs += [pl.BlockSpec((1, D_MODEL), fixed), pl.BlockSpec((D_MODEL, LANES), fixed),
                 pl.BlockSpec((D_MODEL, LANES), fixed), pl.BlockSpec((1, LANES), fixed),
                 pl.BlockSpec((tm, tm), fixed)]
    return pl.pallas_call(
        functools.partial(_mix_out_kernel, n_act=len(acts), tm=tm),
        grid=(n_tok // tm,),
        in_specs=in_specs,
        out_specs=[pl.BlockSpec((tm, D_MODEL), row), pl.BlockSpec((SUBLANES, tm), col),
                   pl.BlockSpec((SUBLANES, tm), col), pl.BlockSpec((LANES, LANES), fixed)],
        out_shape=[jax.ShapeDtypeStruct((n_tok, D_MODEL), F32),
                   jax.ShapeDtypeStruct((SUBLANES, n_tok), jnp.int32),
                   jax.ShapeDtypeStruct((SUBLANES, n_tok), F32),
                   jax.ShapeDtypeStruct((LANES, LANES), F32)],
        scratch_shapes=[pltpu.VMEM((LANES, LANES), F32)],
        compiler_params=_params("arbitrary"),
        name="mix_out_router",
    )(h, *acts, *ws, nw, wr_hi, wr_lo, br, triu)


def _ffn_kernel(nused_ref, ea_ref, eb_ref, tok_ref, tokn_ref, g_ref, nw_ref,
                wga_ref, wua_ref, wda_ref, wgb_ref, wub_ref, wdb_ref, fw_ref, h_hbm,
                out_hbm, xbuf, obuf, gsem, ssem, *, bm, n_tok, n_blocks, final_norm):
    del ea_ref, eb_ref
    i = pl.program_id(0)
    slot = i % 2
    nused = nused_ref[0]

    def row_copy_in(tref, s, r):
        t = jnp.minimum(tref[0, 0, r], n_tok - 1)
        return pltpu.make_async_copy(h_hbm.at[pl.ds(t, 1)], xbuf.at[s, pl.ds(r, 1)], gsem.at[s])

    def row_copy_out(s, r):
        t = tok_ref[0, 0, r]
        t = jnp.where(t >= n_tok, n_tok + s * bm + r, t)
        return pltpu.make_async_copy(obuf.at[s, pl.ds(r, 1)], out_hbm.at[pl.ds(t, 1)], ssem.at[s])

    def gather(tref, s):
        def body(r, c):
            row_copy_in(tref, s, r).start()
            return c
        lax.fori_loop(0, bm, body, 0, unroll=8)

    def wait_gather(s):
        pltpu.make_async_copy(h_hbm.at[pl.ds(0, bm)], xbuf.at[s], gsem.at[s]).wait()

    def wait_scatter(s):
        pltpu.make_async_copy(obuf.at[s], out_hbm.at[pl.ds(0, bm)], ssem.at[s]).wait()

    @pl.when(i == 0)
    def _():
        gather(tok_ref, 0)
        obuf[...] = jnp.zeros(obuf.shape, F32)
        for s in range(2):
            cp = pltpu.make_async_copy(obuf.at[s], out_hbm.at[pl.ds(n_tok + s * bm, bm)], ssem.at[s])
            cp.start()
            cp.wait()

    @pl.when(i + 1 < nused)
    def _():
        gather(tokn_ref, 1 - slot)

    @pl.when((i >= 2) & (i - 2 < nused))
    def _():
        wait_scatter(slot)

    @pl.when(i < nused)
    def _():
        wait_gather(slot)
        x = xbuf[slot]
        xb = _rms(x, nw_ref[...]).astype(BF16)
        g = g_ref[...]

        def expert(wg_ref, wu_ref, wd_ref, gate):
            hid = jax.nn.silu(_dot(xb, wg_ref[0])) * _dot(xb, wu_ref[0]) * gate
            return _dot(hid.astype(BF16), wd_ref[0])

        y = x + expert(wga_ref, wua_ref, wda_ref, g[:, 0:1]) + expert(wgb_ref, wub_ref, wdb_ref, g[:, 1:2])
        if final_norm:
            y = _rms(y, fw_ref[...])
        obuf[slot] = y

        def body(r, c):
            row_copy_out(slot, r).start()
            return c
        lax.fori_loop(0, bm, body, 0, unroll=8)

    @pl.when(i == n_blocks - 1)
    def _():
        @pl.when((i >= 1) & (i - 1 < nused))
        def _():
            wait_scatter(1 - slot)

        @pl.when(i < nused)
        def _():
            wait_scatter(slot)


def _moe_ffn(h, n_tok, nused, blk_ea, blk_eb, tok_rows, gate_rows, nw, wg, wu, wd, fw, final_norm):
    bm = MOE_BLOCK
    n_blocks = tok_rows.shape[0]
    tok_spec = pl.BlockSpec((1, 1, bm), lambda i, nu, ea, eb: (i, 0, 0), memory_space=pltpu.SMEM)
    tokn_spec = pl.BlockSpec((1, 1, bm), lambda i, nu, ea, eb: (jnp.minimum(i + 1, n_blocks - 1), 0, 0),
                             memory_space=pltpu.SMEM)
    fixed = lambda i, nu, ea, eb: (0, 0)
    wa = lambda i, nu, ea, eb: (ea[i], 0, 0)
    wb = lambda i, nu, ea, eb: (eb[i], 0, 0)
    up = pl.BlockSpec((1, D_MODEL, EXPERT_FF), wa)
    upb = pl.BlockSpec((1, D_MODEL, EXPERT_FF), wb)
    grid_spec = pltpu.PrefetchScalarGridSpec(
        num_scalar_prefetch=3,
        grid=(n_blocks,),
        in_specs=[tok_spec, tokn_spec,
                  pl.BlockSpec((bm, 2), lambda i, nu, ea, eb: (i, 0)),
                  pl.BlockSpec((1, D_MODEL), fixed),
                  up, up, pl.BlockSpec((1, EXPERT_FF, D_MODEL), wa),
                  upb, upb, pl.BlockSpec((1, EXPERT_FF, D_MODEL), wb),
                  pl.BlockSpec((1, D_MODEL), fixed),
                  pl.BlockSpec(memory_space=pl.ANY)],
        out_specs=pl.BlockSpec(memory_space=pl.ANY),
        scratch_shapes=[pltpu.VMEM((2, bm, D_MODEL), F32), pltpu.VMEM((2, bm, D_MODEL), F32),
                        pltpu.SemaphoreType.DMA((2,)), pltpu.SemaphoreType.DMA((2,))],
    )
    return pl.pallas_call(
        functools.partial(_ffn_kernel, bm=bm, n_tok=n_tok, n_blocks=n_blocks, final_norm=final_norm),
        grid_spec=grid_spec,
        out_shape=jax.ShapeDtypeStruct((n_tok + 2 * bm, D_MODEL), F32),
        compiler_params=_params("arbitrary"),
        name="moe_ffn",
    )(nused, blk_ea, blk_eb, tok_rows, tok_rows, gate_rows, nw, wg, wu, wd, wg, wu, wd, fw, h)


_PAIR_A = np.array([a for a in range(EXPERTS_PER_GROUP) for b in range(a + 1, EXPERTS_PER_GROUP)], np.int32)
_PAIR_B = np.array([b for a in range(EXPERTS_PER_GROUP) for b in range(a + 1, EXPERTS_PER_GROUP)], np.int32)


def _moe_layer(h, n_tok, route_i, route_f, counts, nw, wg, wu, wd, fw, final_norm):
    bm = MOE_BLOCK
    n_blocks = n_tok // bm + N_BUCKETS
    bucket = route_i[0]
    rank = route_i[1]
    cnt = counts[:N_BUCKETS, 0].astype(jnp.int32)
    nblk = (cnt + bm - 1) // bm
    blk_end = jnp.cumsum(nblk)
    blk_start = blk_end - nblk
    dest = blk_start[bucket] * bm + rank
    tok_rows = jnp.full((n_blocks * bm,), n_tok, jnp.int32).at[dest].set(
        jnp.arange(n_tok, dtype=jnp.int32), unique_indices=True)
    blk_bucket = jnp.minimum(
        jnp.searchsorted(blk_end, jnp.arange(n_blocks, dtype=jnp.int32), side='right'),
        N_BUCKETS - 1).astype(jnp.int32)
    nused = blk_end[-1:].astype(jnp.int32)
    last = blk_bucket[jnp.maximum(nused[0] - 1, 0)]
    blk_bucket = jnp.where(jnp.arange(n_blocks) < nused[0], blk_bucket, last)
    grp = blk_bucket // N_PAIRS
    pair = blk_bucket % N_PAIRS
    blk_ea = grp * EXPERTS_PER_GROUP + jnp.asarray(_PAIR_A)[pair]
    blk_eb = grp * EXPERTS_PER_GROUP + jnp.asarray(_PAIR_B)[pair]
    gates = jnp.concatenate([route_f[0:2], jnp.zeros((2, 1), F32)], axis=1)
    gate_rows = gates[:, tok_rows].T
    return _moe_ffn(h, n_tok, nused, blk_ea, blk_eb, tok_rows.reshape(n_blocks, 1, bm), gate_rows,
                    nw, wg, wu, wd, fw, final_norm)


def _odd_in_kernel(h_ref, nw_ref, wz_ref, wx_ref, wdt_ref, wdtt_ref, cw_ref, cb_ref, dtb_ref, dtbt_ref,
                   z_ref, x_ref, b_ref, c_ref, dt_ref, dtt_ref, ubuf, *, tm, tiles_per_seq):
    i = pl.program_id(0)
    xb = _rms(h_ref[...], nw_ref[...]).astype(BF16)
    cw_all = cw_ref[...]
    cb_all = cb_ref[...]
    wchunk = 512

    @pl.when(i % tiles_per_seq == 0)
    def _():
        ubuf[0:SUBLANES, :] = jnp.zeros((SUBLANES, SSM_CONV_DIM), F32)

    for c0 in range(0, D_INNER, wchunk):
        z_ref[:, c0:c0 + wchunk] = _dot(xb, wz_ref[:, c0:c0 + wchunk]).astype(BF16)

    for c0 in range(0, SSM_CONV_DIM, wchunk):
        cols = slice(c0, c0 + wchunk)
        u = _dot(xb, wx_ref[:, cols])
        ubuf[SUBLANES:SUBLANES + tm, cols] = u
        cw = cw_all[:, cols]
        conv = cw[3:4, :] * u + cb_all[:, cols]
        for j in range(1, SSM_CONV_K):
            conv = conv + cw[3 - j:4 - j, :] * ubuf[SUBLANES - j:SUBLANES - j + tm, cols]
        act = jax.nn.silu(conv).astype(BF16)
        if c0 < D_INNER:
            x_ref[:, c0:c0 + wchunk] = act
        elif c0 < D_INNER + SSM_GN:
            b_ref[:, c0 - D_INNER:c0 - D_INNER + wchunk] = act
        else:
            c_ref[:, c0 - D_INNER - SSM_GN:c0 - D_INNER - SSM_GN + wchunk] = act
    ubuf[0:SUBLANES, :] = ubuf[tm:tm + SUBLANES, :]

    dt_ref[...] = jax.nn.softplus(_dot(xb, wdt_ref[...]) + dtb_ref[...])
    dtt_ref[...] = jax.nn.softplus(_dot_nt(wdtt_ref[...], xb) + dtbt_ref[...])


def _odd_in_proj(h, n_tok, seq, nw, wz, wx, wdt, wdtt, conv_w, conv_b, dtb, dtbt):
    tm = ROW_TILE
    row = lambda i: (i, 0)
    col = lambda i: (0, i)
    fixed = lambda i: (0, 0)
    full = lambda a: pl.BlockSpec(a.shape, fixed)
    return pl.pallas_call(
        functools.partial(_odd_in_kernel, tm=tm, tiles_per_seq=seq // tm),
        grid=(n_tok // tm,),
        in_specs=[pl.BlockSpec((tm, D_MODEL), row), full(nw), full(wz), full(wx), full(wdt), full(wdtt),
                  full(conv_w), full(conv_b), full(dtb), full(dtbt)],
        out_specs=[pl.BlockSpec((tm, D_INNER), row), pl.BlockSpec((tm, D_INNER), row),
                   pl.BlockSpec((tm, SSM_GN), row), pl.BlockSpec((tm, SSM_GN), row),
                   pl.BlockSpec((tm, SSM_HEADS), row), pl.BlockSpec((SSM_HEADS, tm), col)],
        out_shape=[jax.ShapeDtypeStruct((n_tok, D_INNER), BF16), jax.ShapeDtypeStruct((n_tok, D_INNER), BF16),
                   jax.ShapeDtypeStruct((n_tok, SSM_GN), BF16), jax.ShapeDtypeStruct((n_tok, SSM_GN), BF16),
                   jax.ShapeDtypeStruct((n_tok, SSM_HEADS), F32), jax.ShapeDtypeStruct((SSM_HEADS, n_tok), F32)],
        scratch_shapes=[pltpu.VMEM((tm + SUBLANES, SSM_CONV_DIM), F32)],
        compiler_params=_params("arbitrary"),
        name="odd_in_proj",
    )(h, nw, wz, wx, wdt, wdtt, conv_w, conv_b, dtb, dtbt)


def _ssd_kernel(x_ref, b_ref, c_ref, z_ref, dt_ref, dtt_ref, alog_ref, alogt_ref, dskip_ref, normw_ref,
                tril_ref, triu_ref, y_ref, state, *, lc):
    ci = pl.program_id(1)

    @pl.when(ci == 0)
    def _():
        state[...] = jnp.zeros(state.shape, F32)

    gw = SSM_HPG * SSM_HEAD_DIM
    hi = lax.Precision.HIGHEST
    dt = dt_ref[...]
    dtt = dtt_ref[...]
    a = -jnp.exp(alog_ref[...])
    at = -jnp.exp(alogt_ref[...])
    cs = jnp.dot(tril_ref[...], dt * a, precision=hi, preferred_element_type=F32)
    cst = jnp.dot(dtt * at, triu_ref[...], precision=hi, preferred_element_type=F32)
    cs_last = cs[lc - 1:lc, :]
    grow = jnp.exp(cs)
    dt_end = dt * jnp.exp(cs_last - cs)
    e_last = jnp.exp(cs_last)
    rr = lax.broadcasted_iota(jnp.int32, (lc, lc), 0)
    cc = lax.broadcasted_iota(jnp.int32, (lc, lc), 1)
    lower = cc <= rr
    lane = lax.broadcasted_iota(jnp.int32, (lc, gw), 1)
    head_of_lane = lane // SSM_HEAD_DIM

    def expand(m, h0):
        out = m[:, h0 + SSM_HPG - 1:h0 + SSM_HPG]
        for k in range(SSM_HPG - 2, -1, -1):
            out = jnp.where(head_of_lane == k, m[:, h0 + k:h0 + k + 1], out)
        return out

    for g in range(SSM_GROUPS):
        h0 = g * SSM_HPG
        bg = b_ref[:, g * SSM_STATE:(g + 1) * SSM_STATE]
        cg = c_ref[:, g * SSM_STATE:(g + 1) * SSM_STATE]
        xg = x_ref[:, g * gw:(g + 1) * gw].astype(F32)
        cb = _dot_nt(cg, bg)
        xdt = xg * expand(dt, h0)
        xdt_b = xdt.astype(BF16)
        st = state[g]
        y = _dot(cg, st.astype(BF16)) * expand(grow, h0)
        for k in range(SSM_HPG):
            h = h0 + k
            seg = cs[:, h:h + 1] - cst[h:h + 1, :]
            decay = jnp.exp(jnp.where(lower, seg, -jnp.inf))
            m = (cb * decay).astype(BF16)
            y = y + _dot(m, jnp.where(head_of_lane == k, xdt_b, jnp.zeros_like(xdt_b)))
        x_end = (xg * expand(dt_end, h0)).astype(BF16)
        bgt = bg.astype(F32).T.astype(BF16)
        state[g] = st * expand(e_last, h0)[0:1, :] + _dot(bgt, x_end)
        y = y + dskip_ref[:, g * gw:(g + 1) * gw] * xg
        y = y * jax.nn.silu(z_ref[:, g * gw:(g + 1) * gw].astype(F32))
        y = y * lax.rsqrt(jnp.mean(y * y, axis=-1, keepdims=True) + RMS_EPS)
        y_ref[:, g * gw:(g + 1) * gw] = (y * normw_ref[:, g * gw:(g + 1) * gw]).astype(BF16)


def _ssd(x, bm, cm, z, dt, dtt, n_tok, seq, alog, alogt, dskip, normw):
    lc = SSD_CHUNK
    nc = seq // lc
    tril = jnp.asarray(np.tril(np.ones((lc, lc), np.float32)))
    triu = jnp.asarray(np.triu(np.ones((lc, lc), np.float32)))
    row = lambda b, c: (b * nc + c, 0)
    col = lambda b, c: (0, b * nc + c)
    fixed = lambda b, c: (0, 0)
    full = lambda a: pl.BlockSpec(a.shape, fixed)
    return pl.pallas_call(
        functools.partial(_ssd_kernel, lc=lc),
        grid=(n_tok // seq, nc),
        in_specs=[pl.BlockSpec((lc, D_INNER), row), pl.BlockSpec((lc, SSM_GN), row),
                  pl.BlockSpec((lc, SSM_GN), row), pl.BlockSpec((lc, D_INNER), row),
                  pl.BlockSpec((lc, SSM_HEADS), row), pl.BlockSpec((SSM_HEADS, lc), col),
                  full(alog), full(alogt), full(dskip), full(normw), full(tril), full(triu)],
        out_specs=pl.BlockSpec((lc, D_INNER), row),
        out_shape=jax.ShapeDtypeStruct((n_tok, D_INNER), BF16),
        scratch_shapes=[pltpu.VMEM((SSM_GROUPS, SSM_STATE, SSM_HPG * SSM_HEAD_DIM), F32)],
        compiler_params=_params("arbitrary", "arbitrary"),
        name="ssd_scan",
    )(x, bm, cm, z, dt, dtt, alog, alogt, dskip, normw, tril, triu)


def _router_weights(wg, bg, we, be):
    w = jnp.concatenate([wg, jnp.moveaxis(we, 0, 1).reshape(D_MODEL, -1)], axis=1)
    b = jnp.concatenate([bg, be.reshape(-1)])
    pad = LANES - w.shape[1]
    w = jnp.pad(w, ((0, 0), (0, pad)))
    b = jnp.pad(b, (0, pad)).reshape(1, LANES)
    w_hi = w.astype(BF16)
    w_lo = (w - w_hi.astype(F32)).astype(BF16)
    return w_hi, w_lo, b


def kernel(x, even_w_in, even_conv_w, even_w_out, odd_in_proj, odd_conv_w, odd_conv_b, odd_dt_bias,
           odd_a_log, odd_d_skip, odd_norm_w, odd_out_proj, norm_mix, norm_ffn, router_g_w, router_g_b,
           router_e_w, router_e_b, expert_w_gate, expert_w_up, expert_w_down, final_norm):
    bsz, seq, d = x.shape
    n_tok = bsz * seq
    depth = norm_mix.shape[0]
    h = x.reshape(n_tok, d)
    fw = final_norm.reshape(1, d)
    for layer in range(depth):
        i = layer // 2
        nw = norm_mix[layer].reshape(1, d)
        if layer % 2 == 0:
            conv, q, k, v = _even_in_proj(h, n_tok, seq, nw, even_w_in[i].astype(BF16), even_conv_w[i])
            sb = _stick_breaking(q, k, v, n_tok, seq)
            w_out = even_w_out[i].astype(BF16)
            acts, ws = [conv, sb], [w_out[:CONV_WIDTH], w_out[CONV_WIDTH:]]
        else:
            w_in = odd_in_proj[i]
            wz = w_in[:, :D_INNER].astype(BF16)
            wx = w_in[:, D_INNER:D_INNER + SSM_CONV_DIM].astype(BF16)
            wdt = w_in[:, D_INNER + SSM_CONV_DIM:].astype(BF16)
            z, xs, bmat, cmat, dt, dtt = _odd_in_proj(
                h, n_tok, seq, nw, wz, wx, wdt, wdt.T, odd_conv_w[i], odd_conv_b[i].reshape(1, -1),
                odd_dt_bias[i].reshape(1, -1), odd_dt_bias[i].reshape(-1, 1))
            y = _ssd(xs, bmat, cmat, z, dt, dtt, n_tok, seq, odd_a_log[i].reshape(1, -1),
                     odd_a_log[i].reshape(-1, 1), jnp.repeat(odd_d_skip[i], SSM_HEAD_DIM).reshape(1, -1),
                     odd_norm_w[i].reshape(1, -1))
            acts, ws = [y], [odd_out_proj[i].astype(BF16)]
        wr_hi, wr_lo, br = _router_weights(router_g_w[layer], router_g_b[layer], router_e_w[layer],
                                           router_e_b[layer])
        h, route_i, route_f, counts = _mix_out_router(h, acts, ws, n_tok, norm_ffn[layer].reshape(1, d),
                                                      wr_hi, wr_lo, br)
        h = _moe_layer(h, n_tok, route_i, route_f, counts, norm_ffn[layer].reshape(1, d),
                       expert_w_gate[layer].astype(BF16), expert_w_up[layer].astype(BF16),
                       expert_w_down[layer].astype(BF16), fw, final_norm=(layer == depth - 1))
    return h[:n_tok].reshape(bsz, seq, d)
```

```python
import functools

import numpy as np
import jax
import jax.numpy as jnp
from jax import lax
from jax.experimental import pallas as pl
from jax.experimental.pallas import tpu as pltpu

F32 = jnp.float32
BF16 = jnp.bfloat16

D_MODEL = 1024
RMS_EPS = 1e-6
CONV_WIDTH = 512
SHORT_CONV_K = 3
SB_HEAD_DIM = 64
SB_WIDTH = 512
D_INNER = 2048
SSM_HEAD_DIM = 64
SSM_HEADS = 32
SSM_GROUPS = 8
SSM_HPG = 4
SSM_STATE = 128
SSM_CONV_K = 4
SSM_GN = SSM_GROUPS * SSM_STATE
SSM_CONV_DIM = D_INNER + 2 * SSM_GN
N_EXPERT_GROUPS = 4
EXPERTS_PER_GROUP = 8
EXPERT_FF = 512
N_PAIRS = EXPERTS_PER_GROUP * (EXPERTS_PER_GROUP - 1) // 2
N_BUCKETS = N_EXPERT_GROUPS * N_PAIRS

LANES = 128
SUBLANES = 8
VMEM_LIMIT = 56 * 1024 * 1024

ROW_TILE = 512
SB_BLOCK = 256
SSD_CHUNK = 128
MOE_BLOCK = 256


def _params(*sem):
    return pltpu.CompilerParams(dimension_semantics=sem, vmem_limit_bytes=VMEM_LIMIT)


def _rms(x, w):
    return x * lax.rsqrt(jnp.mean(x * x, axis=-1, keepdims=True) + RMS_EPS) * w


def _dot(a, b):
    return jnp.dot(a, b, preferred_element_type=F32)


def _dot_nt(a, b):
    return lax.dot_general(a, b, (((1,), (1,)), ((), ())), preferred_element_type=F32)


def _even_in_kernel(h_ref, nw_ref, w_ref, cw_ref, conv_ref, q_ref, k_ref, v_ref, ubuf,
                    *, tm, tiles_per_seq):
    i = pl.program_id(0)
    xb = _rms(h_ref[...], nw_ref[...]).astype(BF16)
    c = CONV_WIDTH

    def proj(c0):
        return _dot(xb, w_ref[:, c0:c0 + c])

    gate_b = proj(0)
    u = proj(c) * proj(2 * c)

    @pl.when(i % tiles_per_seq == 0)
    def _():
        ubuf[0:SUBLANES, :] = jnp.zeros((SUBLANES, c), F32)

    ubuf[SUBLANES:SUBLANES + tm, :] = u
    cw = cw_ref[...]
    conv = cw[2:3, :] * u + cw[1:2, :] * ubuf[SUBLANES - 1:SUBLANES - 1 + tm, :] \
        + cw[0:1, :] * ubuf[SUBLANES - 2:SUBLANES - 2 + tm, :]
    conv_ref[...] = (gate_b * conv).astype(BF16)
    ubuf[0:SUBLANES, :] = ubuf[tm:tm + SUBLANES, :]
    q_ref[...] = (proj(3 * c) * (SB_HEAD_DIM ** -0.5)).astype(BF16)
    k_ref[...] = proj(4 * c).astype(BF16)
    v_ref[...] = proj(5 * c).astype(BF16)


def _even_in_proj(h, n_tok, seq, nw, w_in, conv_w):
    tm = ROW_TILE
    c = CONV_WIDTH
    row = lambda i: (i, 0)
    fixed = lambda i: (0, 0)
    out = jax.ShapeDtypeStruct((n_tok, c), BF16)
    return pl.pallas_call(
        functools.partial(_even_in_kernel, tm=tm, tiles_per_seq=seq // tm),
        grid=(n_tok // tm,),
        in_specs=[pl.BlockSpec((tm, D_MODEL), row), pl.BlockSpec((1, D_MODEL), fixed),
                  pl.BlockSpec((D_MODEL, 6 * c), fixed), pl.BlockSpec((SHORT_CONV_K, c), fixed)],
        out_specs=[pl.BlockSpec((tm, c), row)] * 4,
        out_shape=[out] * 4,
        scratch_shapes=[pltpu.VMEM((tm + SUBLANES, c), F32)],
        compiler_params=_params("arbitrary"),
        name="even_in_proj",
    )(h, nw, w_in, conv_w)


def _sb_kernel(q_ref, k_ref, v_ref, tri_ref, o_ref, *, seq, bq):
    nq = seq // bq
    lane = lax.broadcasted_iota(jnp.int32, (bq, LANES), 1)
    rr = lax.broadcasted_iota(jnp.int32, (bq, bq), 0)
    cc = lax.broadcasted_iota(jnp.int32, (bq, bq), 1)
    causal = cc < rr
    every = cc >= 0
    tri = tri_ref[...]

    def block(qs, kb, vb, run, acc, mask):
        z = _dot_nt(qs, kb)
        sp = jnp.maximum(z, 0.0) + jnp.log(1.0 + jnp.exp(-jnp.abs(z)))
        spm = sp if mask is None else jnp.where(mask, sp, 0.0)
        suffix = _dot(spm.astype(BF16), tri) + run
        p = jnp.exp((z - sp) + suffix)
        if mask is not None:
            p = jnp.where(mask, p, 0.0)
        acc = acc + _dot(p.astype(BF16), vb)
        run = run - jnp.sum(spm, axis=1, keepdims=True)
        return run, acc

    def kv(kj):
        rows = pl.ds(pl.multiple_of(kj * bq, bq), bq)
        return k_ref[rows, :], v_ref[rows, :]

    def heads(qi):
        q = q_ref[qi * bq:(qi + 1) * bq, :]
        return [jnp.where((lane < SB_HEAD_DIM) == (head == 0), q, jnp.zeros_like(q)) for head in range(2)]

    def store(qi, acc):
        o_ref[qi * bq:(qi + 1) * bq, :] = jnp.where(lane < SB_HEAD_DIM, acc[:bq], acc[bq:]).astype(BF16)

    for qa in range(0, nq, 2):
        qb = qa + 1
        qs_b = jnp.concatenate(heads(qb), axis=0)
        qs = jnp.concatenate(heads(qa) + [qs_b], axis=0)
        kb, vb = kv(qb)
        run_b, acc_b = block(qs_b, kb, vb, jnp.zeros((2 * bq, 1), F32), jnp.zeros((2 * bq, LANES), F32),
                             jnp.concatenate([causal, causal], axis=0))
        kb, vb = kv(qa)
        run = jnp.concatenate([jnp.zeros((2 * bq, 1), F32), run_b], axis=0)
        acc = jnp.concatenate([jnp.zeros((2 * bq, LANES), F32), acc_b], axis=0)
        run, acc = block(qs, kb, vb, run, acc, jnp.concatenate([causal, causal, every, every], axis=0))
        if qa > 0:
            def body(it, cr, qs=qs, qa=qa):
                kb, vb = kv(qa - 1 - it)
                return block(qs, kb, vb, cr[0], cr[1], None)
            run, acc = lax.fori_loop(0, qa, body, (run, acc))
        store(qa, acc[:2 * bq])
        store(qb, acc[2 * bq:])


def _stick_breaking(q, k, v, n_tok, seq):
    bq = SB_BLOCK
    tri = jnp.asarray(-np.tril(np.ones((bq, bq), np.float32), -1), BF16)
    blk = pl.BlockSpec((seq, LANES), lambda b, hp: (b, hp))
    return pl.pallas_call(
        functools.partial(_sb_kernel, seq=seq, bq=bq),
        grid=(n_tok // seq, SB_WIDTH // LANES),
        in_specs=[blk, blk, blk, pl.BlockSpec((bq, bq), lambda b, hp: (0, 0))],
        out_specs=blk,
        out_shape=jax.ShapeDtypeStruct((n_tok, SB_WIDTH), BF16),
        compiler_params=_params("arbitrary", "arbitrary"),
        name="stick_breaking",
    )(q, k, v, tri)


def _mix_out_kernel(*refs, n_act, tm):
    h_ref = refs[0]
    act_refs = refs[1:1 + n_act]
    w_refs = refs[1 + n_act:1 + 2 * n_act]
    nw_ref, wrh_ref, wrl_ref, br_ref, triu_ref = refs[1 + 2 * n_act:6 + 2 * n_act]
    hout_ref, ri_ref, rf_ref, cnt_ref, carry = refs[6 + 2 * n_act:]
    i = pl.program_id(0)

    @pl.when(i == 0)
    def _():
        carry[...] = jnp.zeros(carry.shape, F32)

    acc = h_ref[...]
    for a_ref, w_ref in zip(act_refs, w_refs):
        acc = acc + _dot(a_ref[...], w_ref[...])
    hout_ref[...] = acc

    xn = _rms(acc, nw_ref[...])
    xh = xn.astype(BF16)
    xl = (xn - xh.astype(F32)).astype(BF16)
    wrh = wrh_ref[...]
    logits = _dot(xh, wrh) + _dot(xl, wrh) + _dot(xh, wrl_ref[...]) + br_ref[...]
    lt = logits.T
    r = lax.broadcasted_iota(jnp.int32, (LANES, tm), 0).astype(F32)
    big = float(LANES)
    neg = -jnp.inf

    def first_argmax(vals):
        m = jnp.max(vals, axis=0, keepdims=True)
        return m, jnp.min(jnp.where(vals == m, r, big), axis=0, keepdims=True)

    gl = jnp.where(r < N_EXPERT_GROUPS, lt, neg)
    gmax, gidx = first_argmax(gl)
    g_top_p = 1.0 / jnp.sum(jnp.exp(gl - gmax), axis=0, keepdims=True)
    lo = N_EXPERT_GROUPS + EXPERTS_PER_GROUP * gidx
    el = jnp.where((r >= lo) & (r < lo + EXPERTS_PER_GROUP), lt, neg)
    m1, i1 = first_argmax(el)
    m2, i2 = first_argmax(jnp.where(r == i1, neg, el))
    t = jnp.exp(m2 - m1)
    g1 = g_top_p / (1.0 + t)
    g2 = g_top_p * t / (1.0 + t)
    e1 = i1 - lo
    e2 = i2 - lo
    ea = jnp.minimum(e1, e2)
    eb = jnp.maximum(e1, e2)
    pair = ea * (2 * EXPERTS_PER_GROUP - 1 - ea) * 0.5 + (eb - ea - 1.0)
    bucket = gidx * N_PAIRS + pair
    first_is_a = e1 < e2
    gate_a = jnp.where(first_is_a, g1, g2)
    gate_b = jnp.where(first_is_a, g2, g1)

    onehot = r == bucket
    cum = _dot(jnp.where(onehot, 1.0, 0.0).astype(BF16), triu_ref[...])
    prev = carry[...]
    rank = jnp.sum(jnp.where(onehot, cum + prev[:, 0:1], 0.0), axis=0, keepdims=True)
    new = prev + jnp.sum(jnp.where(onehot, 1.0, 0.0), axis=1, keepdims=True)
    carry[...] = new
    cnt_ref[...] = new

    row8 = lax.broadcasted_iota(jnp.int32, (SUBLANES, tm), 0)
    ri_ref[...] = jnp.where(row8 == 0, bucket, jnp.where(row8 == 1, rank, 0.0)).astype(jnp.int32)
    rf_ref[...] = jnp.where(row8 == 0, gate_a, jnp.where(row8 == 1, gate_b, 0.0))


def _mix_out_router(h, acts, ws, n_tok, nw, wr_hi, wr_lo, br):
    tm = ROW_TILE
    row = lambda i: (i, 0)
    col = lambda i: (0, i)
    fixed = lambda i: (0, 0)
    triu = jnp.asarray(np.triu(np.ones((tm, tm), np.float32), 1), BF16)
    in_specs = [pl.BlockSpec((tm, D_MODEL), row)]
    in_specs += [pl.BlockSpec((tm, a.shape[1]), row) for a in acts]
    in_specs += [pl.BlockSpec(w.shape, fixed) for w in ws]
    in_specs += [pl.BlockSpec((1, D_MODEL), fixed), pl.BlockSpec((D_MODEL, LANES), fixed),
                 pl.BlockSpec((D_MODEL, LANES), fixed), pl.BlockSpec((1, LANES), fixed),
                 pl.BlockSpec((tm, tm), fixed)]
    return pl.pallas_call(
        functools.partial(_mix_out_kernel, n_act=len(acts), tm=tm),
        grid=(n_tok // tm,),
        in_specs=in_specs,
        out_specs=[pl.BlockSpec((tm, D_MODEL), row), pl.BlockSpec((SUBLANES, tm), col),
                   pl.BlockSpec((SUBLANES, tm), col), pl.BlockSpec((LANES, LANES), fixed)],
        out_shape=[jax.ShapeDtypeStruct((n_tok, D_MODEL), F32),
                   jax.ShapeDtypeStruct((SUBLANES, n_tok), jnp.int32),
                   jax.ShapeDtypeStruct((SUBLANES, n_tok), F32),
                   jax.ShapeDtypeStruct((LANES, LANES), F32)],
        scratch_shapes=[pltpu.VMEM((LANES, LANES), F32)],
        compiler_params=_params("arbitrary"),
        name="mix_out_router",
    )(h, *acts, *ws, nw, wr_hi, wr_lo, br, triu)


def _ffn_kernel(nused_ref, ea_ref, eb_ref, gcur_ref, gnext_ref, sprev_ref, g_ref, nw_ref,
                wga_ref, wua_ref, wda_ref, wgb_ref, wub_ref, wdb_ref, fw_ref, h_hbm,
                out_hbm, xbuf0, xbuf1, obuf0, obuf1, gsem, ssem, *, bm, n_tok, final_norm):
    del ea_ref, eb_ref
    i = pl.program_id(0)
    slot = i % 2
    nused = nused_ref[0]
    xbufs = (xbuf0, xbuf1)
    obufs = (obuf0, obuf1)

    def row_in(gref, s, r):
        return pltpu.make_async_copy(h_hbm.at[pl.ds(gref[0, 0, r], 1)], xbufs[s].at[pl.ds(r, 1)], gsem.at[s])

    def row_out(s, r):
        return pltpu.make_async_copy(obufs[s].at[pl.ds(r, 1)], out_hbm.at[pl.ds(sprev_ref[0, 0, r], 1)],
                                     ssem.at[s])

    def wait_gather(s):
        pltpu.make_async_copy(h_hbm.at[pl.ds(0, bm)], xbufs[s], gsem.at[s]).wait()

    def wait_scatter(s):
        pltpu.make_async_copy(obufs[s], out_hbm.at[pl.ds(0, bm)], ssem.at[s]).wait()

    @pl.when(i == 0)
    def _():
        def body(r, c):
            row_in(gcur_ref, 0, r).start()
            return c
        lax.fori_loop(0, bm, body, 0, unroll=8)
        for s in range(2):
            obufs[s][...] = jnp.zeros((bm, D_MODEL), F32)
            cp = pltpu.make_async_copy(obufs[s], out_hbm.at[pl.ds(n_tok + s * bm, bm)], ssem.at[s])
            cp.start()
            if s == 1:
                cp.wait()

    def run_block(cur, nxt):
        wait_gather(cur)
        for r in range(bm):
            row_in(gnext_ref, nxt, r).start()
        x = xbufs[cur][...]
        xb = _rms(x, nw_ref[...]).astype(BF16)
        g = g_ref[...]

        def hidden(wg_ref, wu_ref, gate):
            return (jax.nn.silu(_dot(xb, wg_ref[0])) * _dot(xb, wu_ref[0]) * gate).astype(BF16)

        hid_a = hidden(wga_ref, wua_ref, g[:, 0:1])
        hid_b = hidden(wgb_ref, wub_ref, g[:, 1:2])
        wait_scatter(cur)
        for r in range(bm):
            row_out(nxt, r).start()
        y = x + _dot(hid_a, wda_ref[0]) + _dot(hid_b, wdb_ref[0])
        if final_norm:
            y = _rms(y, fw_ref[...])
        obufs[cur][...] = y

    for s in range(2):
        @pl.when((i < nused) & (slot == s))
        def _(s=s):
            run_block(s, 1 - s)

    @pl.when(i == nused)
    def _():
        for s in range(2):
            @pl.when(slot == s)
            def _(s=s):
                wait_gather(s)
                wait_scatter(s)

                def body(r, c):
                    row_out(1 - s, r).start()
                    return c
                lax.fori_loop(0, bm, body, 0, unroll=8)
                wait_scatter(1 - s)


def _moe_ffn(h, n_tok, nused, blk_ea, blk_eb, gather_rows, scatter_rows, gate_rows, nw, wg, wu, wd, fw,
             final_norm):
    bm = MOE_BLOCK
    n_blocks = gather_rows.shape[0]
    cur = lambda i, nu, ea, eb: (i, 0, 0)
    nxt = lambda i, nu, ea, eb: (jnp.minimum(i + 1, n_blocks - 1), 0, 0)
    idx_spec = lambda imap: pl.BlockSpec((1, 1, bm), imap, memory_space=pltpu.SMEM)
    fixed = lambda i, nu, ea, eb: (0, 0)
    wa = lambda i, nu, ea, eb: (ea[i], 0, 0)
    wb = lambda i, nu, ea, eb: (eb[i], 0, 0)
    up = pl.BlockSpec((1, D_MODEL, EXPERT_FF), wa)
    upb = pl.BlockSpec((1, D_MODEL, EXPERT_FF), wb)
    grid_spec = pltpu.PrefetchScalarGridSpec(
        num_scalar_prefetch=3,
        grid=(n_blocks,),
        in_specs=[idx_spec(cur), idx_spec(nxt), idx_spec(cur),
                  pl.BlockSpec((bm, 2), lambda i, nu, ea, eb: (i, 0)),
                  pl.BlockSpec((1, D_MODEL), fixed),
                  up, up, pl.BlockSpec((1, EXPERT_FF, D_MODEL), wa),
                  upb, upb, pl.BlockSpec((1, EXPERT_FF, D_MODEL), wb),
                  pl.BlockSpec((1, D_MODEL), fixed),
                  pl.BlockSpec(memory_space=pl.ANY)],
        out_specs=pl.BlockSpec(memory_space=pl.ANY),
        scratch_shapes=[pltpu.VMEM((bm, D_MODEL), F32)] * 4
        + [pltpu.SemaphoreType.DMA((2,)), pltpu.SemaphoreType.DMA((2,))],
    )
    return pl.pallas_call(
        functools.partial(_ffn_kernel, bm=bm, n_tok=n_tok, final_norm=final_norm),
        grid_spec=grid_spec,
        out_shape=jax.ShapeDtypeStruct((n_tok + 2 * bm, D_MODEL), F32),
        compiler_params=_params("arbitrary"),
        name="moe_ffn",
    )(nused, blk_ea, blk_eb, gather_rows, gather_rows, scatter_rows, gate_rows, nw, wg, wu, wd, wg, wu, wd,
      fw, h)


_PAIR_A = np.array([a for a in range(EXPERTS_PER_GROUP) for b in range(a + 1, EXPERTS_PER_GROUP)], np.int32)
_PAIR_B = np.array([b for a in range(EXPERTS_PER_GROUP) for b in range(a + 1, EXPERTS_PER_GROUP)], np.int32)


def _moe_layer(h, n_tok, route_i, route_f, counts, nw, wg, wu, wd, fw, final_norm):
    bm = MOE_BLOCK
    n_blocks = n_tok // bm + N_BUCKETS + 1
    bucket = route_i[0]
    rank = route_i[1]
    cnt = counts[:N_BUCKETS, 0].astype(jnp.int32)
    nblk = (cnt + bm - 1) // bm
    blk_end = jnp.cumsum(nblk)
    blk_start = blk_end - nblk
    in_bucket = bucket[:, None] == jnp.arange(N_BUCKETS, dtype=jnp.int32)[None, :]
    dest = jnp.sum(jnp.where(in_bucket, blk_start[None, :], 0), axis=1) * bm + rank
    upd = jnp.stack([jnp.arange(n_tok, dtype=jnp.int32), lax.bitcast_convert_type(route_f[0], jnp.int32),
                     lax.bitcast_convert_type(route_f[1], jnp.int32)], axis=1)
    init = jnp.broadcast_to(jnp.array([n_tok, 0, 0], jnp.int32), (n_blocks * bm, 3))
    packed = init.at[dest].set(upd, unique_indices=True)
    tok_rows = packed[:, 0]
    gate_rows = lax.bitcast_convert_type(packed[:, 1:3], F32)
    blk_bucket = jnp.minimum(
        jnp.searchsorted(blk_end, jnp.arange(n_blocks, dtype=jnp.int32), side='right'),
        N_BUCKETS - 1).astype(jnp.int32)
    nused = blk_end[-1:].astype(jnp.int32)
    last = blk_bucket[jnp.maximum(nused[0] - 1, 0)]
    blk_bucket = jnp.where(jnp.arange(n_blocks) < nused[0], blk_bucket, last)
    grp = blk_bucket // N_PAIRS
    pair = blk_bucket % N_PAIRS
    blk_ea = grp * EXPERTS_PER_GROUP + jnp.asarray(_PAIR_A)[pair]
    blk_eb = grp * EXPERTS_PER_GROUP + jnp.asarray(_PAIR_B)[pair]
    row = jnp.arange(n_blocks * bm, dtype=jnp.int32)
    gather_rows = jnp.minimum(tok_rows, n_tok - 1)
    scatter_rows = jnp.where(tok_rows >= n_tok, n_tok + (row // bm) % 2 * bm + row % bm, tok_rows)
    scatter_rows = jnp.concatenate([n_tok + bm + row[:bm], scatter_rows])
    return _moe_ffn(h, n_tok, nused, blk_ea, blk_eb, gather_rows.reshape(n_blocks, 1, bm),
                    scatter_rows.reshape(n_blocks + 1, 1, bm), gate_rows, nw, wg, wu, wd, fw, final_norm)


def _odd_in_kernel(h_ref, nw_ref, wz_ref, wx_ref, wdt_ref, wdtt_ref, cw_ref, cb_ref, dtb_ref, dtbt_ref,
                   z_ref, x_ref, b_ref, c_ref, dt_ref, dtt_ref, ubuf, *, tm, tiles_per_seq):
    i = pl.program_id(0)
    xb = _rms(h_ref[...], nw_ref[...]).astype(BF16)
    cw_all = cw_ref[...]
    cb_all = cb_ref[...]
    wchunk = 512

    @pl.when(i % tiles_per_seq == 0)
    def _():
        ubuf[0:SUBLANES, :] = jnp.zeros((SUBLANES, SSM_CONV_DIM), F32)

    for c0 in range(0, D_INNER, wchunk):
        z_ref[:, c0:c0 + wchunk] = _dot(xb, wz_ref[:, c0:c0 + wchunk]).astype(BF16)

    for c0 in range(0, SSM_CONV_DIM, wchunk):
        cols = slice(c0, c0 + wchunk)
        u = _dot(xb, wx_ref[:, cols])
        ubuf[SUBLANES:SUBLANES + tm, cols] = u
        cw = cw_all[:, cols]
        conv = cw[3:4, :] * u + cb_all[:, cols]
        for j in range(1, SSM_CONV_K):
            conv = conv + cw[3 - j:4 - j, :] * ubuf[SUBLANES - j:SUBLANES - j + tm, cols]
        act = jax.nn.silu(conv).astype(BF16)
        if c0 < D_INNER:
            x_ref[:, c0:c0 + wchunk] = act
        elif c0 < D_INNER + SSM_GN:
            b_ref[:, c0 - D_INNER:c0 - D_INNER + wchunk] = act
        else:
            c_ref[:, c0 - D_INNER - SSM_GN:c0 - D_INNER - SSM_GN + wchunk] = act
    ubuf[0:SUBLANES, :] = ubuf[tm:tm + SUBLANES, :]

    dt_ref[...] = jax.nn.softplus(_dot(xb, wdt_ref[...]) + dtb_ref[...])
    dtt_ref[...] = jax.nn.softplus(_dot_nt(wdtt_ref[...], xb) + dtbt_ref[...])


def _odd_in_proj(h, n_tok, seq, nw, wz, wx, wdt, wdtt, conv_w, conv_b, dtb, dtbt):
    tm = ROW_TILE
    row = lambda i: (i, 0)
    col = lambda i: (0, i)
    fixed = lambda i: (0, 0)
    full = lambda a: pl.BlockSpec(a.shape, fixed)
    return pl.pallas_call(
        functools.partial(_odd_in_kernel, tm=tm, tiles_per_seq=seq // tm),
        grid=(n_tok // tm,),
        in_specs=[pl.BlockSpec((tm, D_MODEL), row), full(nw), full(wz), full(wx), full(wdt), full(wdtt),
                  full(conv_w), full(conv_b), full(dtb), full(dtbt)],
        out_specs=[pl.BlockSpec((tm, D_INNER), row), pl.BlockSpec((tm, D_INNER), row),
                   pl.BlockSpec((tm, SSM_GN), row), pl.BlockSpec((tm, SSM_GN), row),
                   pl.BlockSpec((tm, SSM_HEADS), row), pl.BlockSpec((SSM_HEADS, tm), col)],
        out_shape=[jax.ShapeDtypeStruct((n_tok, D_INNER), BF16), jax.ShapeDtypeStruct((n_tok, D_INNER), BF16),
                   jax.ShapeDtypeStruct((n_tok, SSM_GN), BF16), jax.ShapeDtypeStruct((n_tok, SSM_GN), BF16),
                   jax.ShapeDtypeStruct((n_tok, SSM_HEADS), F32), jax.ShapeDtypeStruct((SSM_HEADS, n_tok), F32)],
        scratch_shapes=[pltpu.VMEM((tm + SUBLANES, SSM_CONV_DIM), F32)],
        compiler_params=_params("arbitrary"),
        name="odd_in_proj",
    )(h, nw, wz, wx, wdt, wdtt, conv_w, conv_b, dtb, dtbt)


def _ssd_kernel(x_ref, b_ref, c_ref, z_ref, dt_ref, dtt_ref, alog_ref, alogt_ref, dskip_ref, normw_ref,
                tril_ref, triu_ref, spread_ref, y_ref, state, *, lc):
    ci = pl.program_id(1)

    @pl.when(ci == 0)
    def _():
        state[...] = jnp.zeros(state.shape, F32)

    gw = SSM_HPG * SSM_HEAD_DIM
    hi = lax.Precision.HIGHEST
    dt = dt_ref[...]
    dtt = dtt_ref[...]
    a = -jnp.exp(alog_ref[...])
    at = -jnp.exp(alogt_ref[...])
    cs = jnp.dot(tril_ref[...], dt * a, precision=hi, preferred_element_type=F32)
    cst = jnp.dot(dtt * at, triu_ref[...], precision=hi, preferred_element_type=F32)
    cs_last = cs[lc - 1:lc, :]
    grow = jnp.exp(cs)
    dt_end = dt * jnp.exp(cs_last - cs)
    e_last = jnp.exp(cs_last)
    rr = lax.broadcasted_iota(jnp.int32, (lc, lc), 0)
    cc = lax.broadcasted_iota(jnp.int32, (lc, lc), 1)
    lower = cc <= rr
    lane = lax.broadcasted_iota(jnp.int32, (lc, gw), 1)
    head_of_lane = lane // SSM_HEAD_DIM
    fac = jnp.concatenate([dt, grow, dt_end, jnp.broadcast_to(e_last, (SUBLANES, SSM_HEADS))], axis=0)
    fac_hi = fac.astype(BF16).astype(F32)
    spread = _dot(jnp.concatenate([fac_hi, fac - fac_hi], axis=1).astype(BF16), spread_ref[...])

    for g in range(SSM_GROUPS):
        h0 = g * SSM_HPG
        bg = b_ref[:, g * SSM_STATE:(g + 1) * SSM_STATE]
        cg = c_ref[:, g * SSM_STATE:(g + 1) * SSM_STATE]
        xg = x_ref[:, g * gw:(g + 1) * gw].astype(F32)
        cb = _dot_nt(cg, bg)
        cols = slice(g * gw, (g + 1) * gw)
        xdt = xg * spread[0:lc, cols]
        xdt_b = xdt.astype(BF16)
        st = state[g]
        y = _dot(cg, st.astype(BF16)) * spread[lc:2 * lc, cols]
        for k in range(SSM_HPG):
            h = h0 + k
            seg = cs[:, h:h + 1] - cst[h:h + 1, :]
            decay = jnp.exp(jnp.where(lower, seg, -jnp.inf))
            m = (cb * decay).astype(BF16)
            y = y + _dot(m, jnp.where(head_of_lane == k, xdt_b, jnp.zeros_like(xdt_b)))
        x_end = (xg * spread[2 * lc:3 * lc, cols]).astype(BF16)
        bgt = bg.astype(F32).T.astype(BF16)
        state[g] = st * spread[3 * lc:3 * lc + 1, cols] + _dot(bgt, x_end)
        y = y + dskip_ref[:, g * gw:(g + 1) * gw] * xg
        y = y * jax.nn.silu(z_ref[:, g * gw:(g + 1) * gw].astype(F32))
        y = y * lax.rsqrt(jnp.mean(y * y, axis=-1, keepdims=True) + RMS_EPS)
        y_ref[:, g * gw:(g + 1) * gw] = (y * normw_ref[:, g * gw:(g + 1) * gw]).astype(BF16)


def _ssd(x, bm, cm, z, dt, dtt, n_tok, seq, alog, alogt, dskip, normw):
    lc = SSD_CHUNK
    nc = seq // lc
    tril = jnp.asarray(np.tril(np.ones((lc, lc), np.float32)))
    triu = jnp.asarray(np.triu(np.ones((lc, lc), np.float32)))
    head_of_col = np.arange(D_INNER) // SSM_HEAD_DIM
    spread = jnp.asarray(np.arange(2 * SSM_HEADS)[:, None] % SSM_HEADS == head_of_col[None, :], BF16)
    row = lambda b, c: (b * nc + c, 0)
    col = lambda b, c: (0, b * nc + c)
    fixed = lambda b, c: (0, 0)
    full = lambda a: pl.BlockSpec(a.shape, fixed)
    return pl.pallas_call(
        functools.partial(_ssd_kernel, lc=lc),
        grid=(n_tok // seq, nc),
        in_specs=[pl.BlockSpec((lc, D_INNER), row), pl.BlockSpec((lc, SSM_GN), row),
                  pl.BlockSpec((lc, SSM_GN), row), pl.BlockSpec((lc, D_INNER), row),
                  pl.BlockSpec((lc, SSM_HEADS), row), pl.BlockSpec((SSM_HEADS, lc), col),
                  full(alog), full(alogt), full(dskip), full(normw), full(tril), full(triu),
                  full(spread)],
        out_specs=pl.BlockSpec((lc, D_INNER), row),
        out_shape=jax.ShapeDtypeStruct((n_tok, D_INNER), BF16),
        scratch_shapes=[pltpu.VMEM((SSM_GROUPS, SSM_STATE, SSM_HPG * SSM_HEAD_DIM), F32)],
        compiler_params=_params("arbitrary", "arbitrary"),
        name="ssd_scan",
    )(x, bm, cm, z, dt, dtt, alog, alogt, dskip, normw, tril, triu, spread)


def _router_weights(wg, bg, we, be):
    w = jnp.concatenate([wg, jnp.moveaxis(we, 0, 1).reshape(D_MODEL, -1)], axis=1)
    b = jnp.concatenate([bg, be.reshape(-1)])
    pad = LANES - w.shape[1]
    w = jnp.pad(w, ((0, 0), (0, pad)))
    b = jnp.pad(b, (0, pad)).reshape(1, LANES)
    w_hi = w.astype(BF16)
    w_lo = (w - w_hi.astype(F32)).astype(BF16)
    return w_hi, w_lo, b


def kernel(x, even_w_in, even_conv_w, even_w_out, odd_in_proj, odd_conv_w, odd_conv_b, odd_dt_bias,
           odd_a_log, odd_d_skip, odd_norm_w, odd_out_proj, norm_mix, norm_ffn, router_g_w, router_g_b,
           router_e_w, router_e_b, expert_w_gate, expert_w_up, expert_w_down, final_norm):
    bsz, seq, d = x.shape
    n_tok = bsz * seq
    depth = norm_mix.shape[0]
    h = x.reshape(n_tok, d)
    fw = final_norm.reshape(1, d)
    for layer in range(depth):
        i = layer // 2
        nw = norm_mix[layer].reshape(1, d)
        if layer % 2 == 0:
            conv, q, k, v = _even_in_proj(h, n_tok, seq, nw, even_w_in[i].astype(BF16), even_conv_w[i])
            sb = _stick_breaking(q, k, v, n_tok, seq)
            w_out = even_w_out[i].astype(BF16)
            acts, ws = [conv, sb], [w_out[:CONV_WIDTH], w_out[CONV_WIDTH:]]
        else:
            w_in = odd_in_proj[i]
            wz = w_in[:, :D_INNER].astype(BF16)
            wx = w_in[:, D_INNER:D_INNER + SSM_CONV_DIM].astype(BF16)
            wdt = w_in[:, D_INNER + SSM_CONV_DIM:].astype(BF16)
            z, xs, bmat, cmat, dt, dtt = _odd_in_proj(
                h, n_tok, seq, nw, wz, wx, wdt, wdt.T, odd_conv_w[i], odd_conv_b[i].reshape(1, -1),
                odd_dt_bias[i].reshape(1, -1), odd_dt_bias[i].reshape(-1, 1))
            y = _ssd(xs, bmat, cmat, z, dt, dtt, n_tok, seq, odd_a_log[i].reshape(1, -1),
                     odd_a_log[i].reshape(-1, 1), jnp.repeat(odd_d_skip[i], SSM_HEAD_DIM).reshape(1, -1),
                     odd_norm_w[i].reshape(1, -1))
            acts, ws = [y], [odd_out_proj[i].astype(BF16)]
        wr_hi, wr_lo, br = _router_weights(router_g_w[layer], router_g_b[layer], router_e_w[layer],
                                           router_e_b[layer])
        h, route_i, route_f, counts = _mix_out_router(h, acts, ws, n_tok, norm_ffn[layer].reshape(1, d),
                                                      wr_hi, wr_lo, br)
        h = _moe_layer(h, n_tok, route_i, route_f, counts, norm_ffn[layer].reshape(1, d),
                       expert_w_gate[layer].astype(BF16), expert_w_up[layer].astype(BF16),
                       expert_w_down[layer].astype(BF16), fw, final_norm=(layer == depth - 1))
    return h[:n_tok].reshape(bsz, seq, d)
```

```python
import functools

import numpy as np
import jax
import jax.numpy as jnp
from jax import lax
from jax.experimental import pallas as pl
from jax.experimental.pallas import tpu as pltpu

F32 = jnp.float32
BF16 = jnp.bfloat16

D_MODEL = 1024
RMS_EPS = 1e-6
CONV_WIDTH = 512
SHORT_CONV_K = 3
SB_HEAD_DIM = 64
SB_WIDTH = 512
D_INNER = 2048
SSM_HEAD_DIM = 64
SSM_HEADS = 32
SSM_GROUPS = 8
SSM_HPG = 4
SSM_STATE = 128
SSM_CONV_K = 4
SSM_GN = SSM_GROUPS * SSM_STATE
SSM_CONV_DIM = D_INNER + 2 * SSM_GN
N_EXPERT_GROUPS = 4
EXPERTS_PER_GROUP = 8
EXPERT_FF = 512
N_PAIRS = EXPERTS_PER_GROUP * (EXPERTS_PER_GROUP - 1) // 2
N_BUCKETS = N_EXPERT_GROUPS * N_PAIRS

LANES = 128
SUBLANES = 8
VMEM_LIMIT = 56 * 1024 * 1024

ROW_TILE = 512
ODD_SUB_TILE = 128
SB_BLOCK = 256
SSD_CHUNK = 128
MOE_BLOCK = 256


def _params(*sem):
    return pltpu.CompilerParams(dimension_semantics=sem, vmem_limit_bytes=VMEM_LIMIT)


def _rms(x, w):
    return x * lax.rsqrt(jnp.mean(x * x, axis=-1, keepdims=True) + RMS_EPS) * w


def _dot(a, b):
    return jnp.dot(a, b, preferred_element_type=F32)


def _dot_nt(a, b):
    return lax.dot_general(a, b, (((1,), (1,)), ((), ())), preferred_element_type=F32)


def _even_in_kernel(h_ref, nw_ref, w_ref, cw_ref, conv_ref, q_ref, k_ref, v_ref, ubuf,
                    *, tm, tiles_per_seq):
    i = pl.program_id(0)
    xb = _rms(h_ref[...], nw_ref[...]).astype(BF16)
    c = CONV_WIDTH

    def proj(c0):
        return _dot(xb, w_ref[:, c0:c0 + c])

    gate_b = proj(0)
    u = proj(c) * proj(2 * c)

    @pl.when(i % tiles_per_seq == 0)
    def _():
        ubuf[0:SUBLANES, :] = jnp.zeros((SUBLANES, c), F32)

    ubuf[SUBLANES:SUBLANES + tm, :] = u
    cw = cw_ref[...]
    conv = cw[2:3, :] * u + cw[1:2, :] * ubuf[SUBLANES - 1:SUBLANES - 1 + tm, :] \
        + cw[0:1, :] * ubuf[SUBLANES - 2:SUBLANES - 2 + tm, :]
    conv_ref[...] = (gate_b * conv).astype(BF16)
    ubuf[0:SUBLANES, :] = ubuf[tm:tm + SUBLANES, :]
    q_ref[...] = (proj(3 * c) * (SB_HEAD_DIM ** -0.5)).astype(BF16)
    k_ref[...] = proj(4 * c).astype(BF16)
    v_ref[...] = proj(5 * c).astype(BF16)


def _even_in_proj(h, n_tok, seq, nw, w_in, conv_w):
    tm = ROW_TILE
    c = CONV_WIDTH
    row = lambda i: (i, 0)
    fixed = lambda i: (0, 0)
    out = jax.ShapeDtypeStruct((n_tok, c), BF16)
    return pl.pallas_call(
        functools.partial(_even_in_kernel, tm=tm, tiles_per_seq=seq // tm),
        grid=(n_tok // tm,),
        in_specs=[pl.BlockSpec((tm, D_MODEL), row), pl.BlockSpec((1, D_MODEL), fixed),
                  pl.BlockSpec((D_MODEL, 6 * c), fixed), pl.BlockSpec((SHORT_CONV_K, c), fixed)],
        out_specs=[pl.BlockSpec((tm, c), row)] * 4,
        out_shape=[out] * 4,
        scratch_shapes=[pltpu.VMEM((tm + SUBLANES, c), F32)],
        compiler_params=_params("arbitrary"),
        name="even_in_proj",
    )(h, nw, w_in, conv_w)


def _sb_kernel(q_ref, k_ref, v_ref, tri_ref, o_ref, *, seq, bq):
    nq = seq // bq
    lane = lax.broadcasted_iota(jnp.int32, (bq, LANES), 1)
    rr = lax.broadcasted_iota(jnp.int32, (bq, bq), 0)
    cc = lax.broadcasted_iota(jnp.int32, (bq, bq), 1)
    causal = cc < rr
    every = cc >= 0
    tri = tri_ref[...]

    def block(qs, kb, vb, run, acc, mask):
        z = _dot_nt(qs, kb)
        sp = jnp.maximum(z, 0.0) + jnp.log(1.0 + jnp.exp(-jnp.abs(z)))
        spm = sp if mask is None else jnp.where(mask, sp, 0.0)
        suffix = _dot(spm.astype(BF16), tri) + run
        p = jnp.exp((z - sp) + suffix)
        if mask is not None:
            p = jnp.where(mask, p, 0.0)
        acc = acc + _dot(p.astype(BF16), vb)
        run = run - jnp.sum(spm, axis=1, keepdims=True)
        return run, acc

    def kv(kj):
        rows = pl.ds(pl.multiple_of(kj * bq, bq), bq)
        return k_ref[rows, :], v_ref[rows, :]

    def heads(qi):
        q = q_ref[qi * bq:(qi + 1) * bq, :]
        return [jnp.where((lane < SB_HEAD_DIM) == (head == 0), q, jnp.zeros_like(q)) for head in range(2)]

    def store(qi, acc):
        o_ref[qi * bq:(qi + 1) * bq, :] = jnp.where(lane < SB_HEAD_DIM, acc[:bq], acc[bq:]).astype(BF16)

    for qa in range(0, nq, 2):
        qb = qa + 1
        qs_b = jnp.concatenate(heads(qb), axis=0)
        qs = jnp.concatenate(heads(qa) + [qs_b], axis=0)
        kb, vb = kv(qb)
        run_b, acc_b = block(qs_b, kb, vb, jnp.zeros((2 * bq, 1), F32), jnp.zeros((2 * bq, LANES), F32),
                             jnp.concatenate([causal, causal], axis=0))
        kb, vb = kv(qa)
        run = jnp.concatenate([jnp.zeros((2 * bq, 1), F32), run_b], axis=0)
        acc = jnp.concatenate([jnp.zeros((2 * bq, LANES), F32), acc_b], axis=0)
        run, acc = block(qs, kb, vb, run, acc, jnp.concatenate([causal, causal, every, every], axis=0))
        if qa > 0:
            def body(it, cr, qs=qs, qa=qa):
                kb, vb = kv(qa - 1 - it)
                return block(qs, kb, vb, cr[0], cr[1], None)
            run, acc = lax.fori_loop(0, qa, body, (run, acc))
        store(qa, acc[:2 * bq])
        store(qb, acc[2 * bq:])


def _stick_breaking(q, k, v, n_tok, seq):
    bq = SB_BLOCK
    tri = jnp.asarray(-np.tril(np.ones((bq, bq), np.float32), -1), BF16)
    blk = pl.BlockSpec((seq, LANES), lambda b, hp: (b, hp))
    return pl.pallas_call(
        functools.partial(_sb_kernel, seq=seq, bq=bq),
        grid=(n_tok // seq, SB_WIDTH // LANES),
        in_specs=[blk, blk, blk, pl.BlockSpec((bq, bq), lambda b, hp: (0, 0))],
        out_specs=blk,
        out_shape=jax.ShapeDtypeStruct((n_tok, SB_WIDTH), BF16),
        compiler_params=_params("arbitrary", "arbitrary"),
        name="stick_breaking",
    )(q, k, v, tri)


def _mix_out_kernel(*refs, n_act, tm):
    h_ref = refs[0]
    act_refs = refs[1:1 + n_act]
    w_refs = refs[1 + n_act:1 + 2 * n_act]
    nw_ref, wrh_ref, wrl_ref, br_ref, triu_ref = refs[1 + 2 * n_act:6 + 2 * n_act]
    hout_ref, ri_ref, rf_ref, cnt_ref, carry = refs[6 + 2 * n_act:]
    i = pl.program_id(0)

    @pl.when(i == 0)
    def _():
        carry[...] = jnp.zeros(carry.shape, F32)

    acc = h_ref[...]
    for a_ref, w_ref in zip(act_refs, w_refs):
        acc = acc + _dot(a_ref[...], w_ref[...])
    hout_ref[...] = acc

    xn = _rms(acc, nw_ref[...])
    xh = xn.astype(BF16)
    xl = (xn - xh.astype(F32)).astype(BF16)
    wrh = wrh_ref[...]
    logits = _dot(xh, wrh) + _dot(xl, wrh) + _dot(xh, wrl_ref[...]) + br_ref[...]
    lt = logits.T
    r = lax.broadcasted_iota(jnp.int32, (LANES, tm), 0).astype(F32)
    big = float(LANES)
    neg = -jnp.inf

    def first_argmax(vals):
        m = jnp.max(vals, axis=0, keepdims=True)
        return m, jnp.min(jnp.where(vals == m, r, big), axis=0, keepdims=True)

    gl = jnp.where(r < N_EXPERT_GROUPS, lt, neg)
    gmax, gidx = first_argmax(gl)
    g_top_p = 1.0 / jnp.sum(jnp.exp(gl - gmax), axis=0, keepdims=True)
    lo = N_EXPERT_GROUPS + EXPERTS_PER_GROUP * gidx
    el = jnp.where((r >= lo) & (r < lo + EXPERTS_PER_GROUP), lt, neg)
    m1, i1 = first_argmax(el)
    m2, i2 = first_argmax(jnp.where(r == i1, neg, el))
    t = jnp.exp(m2 - m1)
    g1 = g_top_p / (1.0 + t)
    g2 = g_top_p * t / (1.0 + t)
    e1 = i1 - lo
    e2 = i2 - lo
    ea = jnp.minimum(e1, e2)
    eb = jnp.maximum(e1, e2)
    pair = ea * (2 * EXPERTS_PER_GROUP - 1 - ea) * 0.5 + (eb - ea - 1.0)
    bucket = gidx * N_PAIRS + pair
    first_is_a = e1 < e2
    gate_a = jnp.where(first_is_a, g1, g2)
    gate_b = jnp.where(first_is_a, g2, g1)

    onehot = r == bucket
    cum = _dot(jnp.where(onehot, 1.0, 0.0).astype(BF16), triu_ref[...])
    prev = carry[...]
    rank = jnp.sum(jnp.where(onehot, cum + prev[:, 0:1], 0.0), axis=0, keepdims=True)
    new = prev + jnp.sum(jnp.where(onehot, 1.0, 0.0), axis=1, keepdims=True)
    carry[...] = new
    cnt_ref[...] = new

    row8 = lax.broadcasted_iota(jnp.int32, (SUBLANES, tm), 0)
    ri_ref[...] = jnp.where(row8 == 0, bucket, jnp.where(row8 == 1, rank, 0.0)).astype(jnp.int32)
    rf_ref[...] = jnp.where(row8 == 0, gate_a, jnp.where(row8 == 1, gate_b, 0.0))


def _mix_out_router(h, acts, ws, n_tok, nw, wr_hi, wr_lo, br):
    tm = ROW_TILE
    row = lambda i: (i, 0)
    col = lambda i: (0, i)
    fixed = lambda i: (0, 0)
    triu = jnp.asarray(np.triu(np.ones((tm, tm), np.float32), 1), BF16)
    in_specs = [pl.BlockSpec((tm, D_MODEL), row)]
    in_specs += [pl.BlockSpec((tm, a.shape[1]), row) for a in acts]
    in_specs += [pl.BlockSpec(w.shape, fixed) for w in ws]
    in_specs += [pl.BlockSpec((1, D_MODEL), fixed), pl.BlockSpec((D_MODEL, LANES), fixed),
                 pl.BlockSpec((D_MODEL, LANES), fixed), pl.BlockSpec((1, LANES), fixed),
                 pl.BlockSpec((tm, tm), fixed)]
    return pl.pallas_call(
        functools.partial(_mix_out_kernel, n_act=len(acts), tm=tm),
        grid=(n_tok // tm,),
        in_specs=in_specs,
        out_specs=[pl.BlockSpec((tm, D_MODEL), row), pl.BlockSpec((SUBLANES, tm), col),
                   pl.BlockSpec((SUBLANES, tm), col), pl.BlockSpec((LANES, LANES), fixed)],
        out_shape=[jax.ShapeDtypeStruct((n_tok, D_MODEL), F32),
                   jax.ShapeDtypeStruct((SUBLANES, n_tok), jnp.int32),
                   jax.ShapeDtypeStruct((SUBLANES, n_tok), F32),
                   jax.ShapeDtypeStruct((LANES, LANES), F32)],
        scratch_shapes=[pltpu.VMEM((LANES, LANES), F32)],
        compiler_params=_params("arbitrary"),
        name="mix_out_router",
    )(h, *acts, *ws, nw, wr_hi, wr_lo, br, triu)


def _ffn_kernel(nused_ref, ea_ref, eb_ref, gcur_ref, gnext_ref, sprev_ref, g_ref, nw_ref,
                wga_ref, wua_ref, wda_ref, wgb_ref, wub_ref, wdb_ref, fw_ref, h_hbm,
                out_hbm, xbuf0, xbuf1, obuf0, obuf1, gsem, ssem, *, bm, n_tok, final_norm):
    del ea_ref, eb_ref
    i = pl.program_id(0)
    slot = i % 2
    nused = nused_ref[0]
    xbufs = (xbuf0, xbuf1)
    obufs = (obuf0, obuf1)

    def row_in(gref, s, r):
        return pltpu.make_async_copy(h_hbm.at[pl.ds(gref[0, 0, r], 1)], xbufs[s].at[pl.ds(r, 1)], gsem.at[s])

    def row_out(s, r):
        return pltpu.make_async_copy(obufs[s].at[pl.ds(r, 1)], out_hbm.at[pl.ds(sprev_ref[0, 0, r], 1)],
                                     ssem.at[s])

    def wait_gather(s):
        pltpu.make_async_copy(h_hbm.at[pl.ds(0, bm)], xbufs[s], gsem.at[s]).wait()

    def wait_scatter(s):
        pltpu.make_async_copy(obufs[s], out_hbm.at[pl.ds(0, bm)], ssem.at[s]).wait()

    @pl.when(i == 0)
    def _():
        def body(r, c):
            row_in(gcur_ref, 0, r).start()
            return c
        lax.fori_loop(0, bm, body, 0, unroll=8)
        for s in range(2):
            obufs[s][...] = jnp.zeros((bm, D_MODEL), F32)
            cp = pltpu.make_async_copy(obufs[s], out_hbm.at[pl.ds(n_tok + s * bm, bm)], ssem.at[s])
            cp.start()
            if s == 1:
                cp.wait()

    def run_block(cur, nxt):
        for r in range(bm):
            row_in(gnext_ref, nxt, r).start()
        wait_gather(cur)
        x = xbufs[cur][...]
        xb = _rms(x, nw_ref[...]).astype(BF16)
        g = g_ref[...]

        def hidden(wg_ref, wu_ref, gate):
            return (jax.nn.silu(_dot(xb, wg_ref[0, 0].astype(BF16))) * _dot(xb, wu_ref[0, 0].astype(BF16))
                    * gate).astype(BF16)

        hid_a = hidden(wga_ref, wua_ref, g[:, 0:1])
        hid_b = hidden(wgb_ref, wub_ref, g[:, 1:2])
        for r in range(bm):
            row_out(nxt, r).start()
        wait_scatter(cur)
        y = x + _dot(hid_a, wda_ref[0, 0].astype(BF16)) + _dot(hid_b, wdb_ref[0, 0].astype(BF16))
        if final_norm:
            y = _rms(y, fw_ref[...])
        obufs[cur][...] = y

    for s in range(2):
        @pl.when((i < nused) & (slot == s))
        def _(s=s):
            run_block(s, 1 - s)

    @pl.when(i == nused)
    def _():
        for s in range(2):
            @pl.when(slot == s)
            def _(s=s):
                wait_gather(s)
                wait_scatter(s)

                def body(r, c):
                    row_out(1 - s, r).start()
                    return c
                lax.fori_loop(0, bm, body, 0, unroll=8)
                wait_scatter(1 - s)


def _moe_ffn(h, n_tok, nused, blk_ea, blk_eb, gather_rows, scatter_rows, gate_rows, nw, layer, wg, wu, wd, fw,
             final_norm):
    bm = MOE_BLOCK
    n_blocks = gather_rows.shape[0]
    cur = lambda i, nu, ea, eb: (i, 0, 0)
    nxt = lambda i, nu, ea, eb: (jnp.minimum(i + 1, n_blocks - 1), 0, 0)
    idx_spec = lambda imap: pl.BlockSpec((1, 1, bm), imap, memory_space=pltpu.SMEM)
    fixed = lambda i, nu, ea, eb: (0, 0)
    wa = lambda i, nu, ea, eb: (layer, ea[i], 0, 0)
    wb = lambda i, nu, ea, eb: (layer, eb[i], 0, 0)
    up = pl.BlockSpec((1, 1, D_MODEL, EXPERT_FF), wa)
    upb = pl.BlockSpec((1, 1, D_MODEL, EXPERT_FF), wb)
    grid_spec = pltpu.PrefetchScalarGridSpec(
        num_scalar_prefetch=3,
        grid=(n_blocks,),
        in_specs=[idx_spec(cur), idx_spec(nxt), idx_spec(cur),
                  pl.BlockSpec((bm, 2), lambda i, nu, ea, eb: (i, 0)),
                  pl.BlockSpec((1, D_MODEL), fixed),
                  up, up, pl.BlockSpec((1, 1, EXPERT_FF, D_MODEL), wa),
                  upb, upb, pl.BlockSpec((1, 1, EXPERT_FF, D_MODEL), wb),
                  pl.BlockSpec((1, D_MODEL), fixed),
                  pl.BlockSpec(memory_space=pl.ANY)],
        out_specs=pl.BlockSpec(memory_space=pl.ANY),
        scratch_shapes=[pltpu.VMEM((bm, D_MODEL), F32)] * 4
        + [pltpu.SemaphoreType.DMA((2,)), pltpu.SemaphoreType.DMA((2,))],
    )
    return pl.pallas_call(
        functools.partial(_ffn_kernel, bm=bm, n_tok=n_tok, final_norm=final_norm),
        grid_spec=grid_spec,
        out_shape=jax.ShapeDtypeStruct((n_tok + 2 * bm, D_MODEL), F32),
        compiler_params=_params("arbitrary"),
        name="moe_ffn",
    )(nused, blk_ea, blk_eb, gather_rows, gather_rows, scatter_rows, gate_rows, nw, wg, wu, wd, wg, wu, wd,
      fw, h)


_PAIR_A = np.array([a for a in range(EXPERTS_PER_GROUP) for b in range(a + 1, EXPERTS_PER_GROUP)], np.int32)
_PAIR_B = np.array([b for a in range(EXPERTS_PER_GROUP) for b in range(a + 1, EXPERTS_PER_GROUP)], np.int32)


def _moe_layer(h, n_tok, route_i, route_f, counts, nw, layer, wg, wu, wd, fw, final_norm):
    bm = MOE_BLOCK
    n_blocks = n_tok // bm + N_BUCKETS + 1
    bucket = route_i[0]
    rank = route_i[1]
    cnt = counts[:N_BUCKETS, 0].astype(jnp.int32)
    nblk = (cnt + bm - 1) // bm
    blk_end = jnp.cumsum(nblk)
    blk_start = blk_end - nblk
    in_bucket = bucket[:, None] == jnp.arange(N_BUCKETS, dtype=jnp.int32)[None, :]
    dest = jnp.sum(jnp.where(in_bucket, blk_start[None, :], 0), axis=1) * bm + rank
    upd = jnp.stack([jnp.arange(n_tok, dtype=jnp.int32), lax.bitcast_convert_type(route_f[0], jnp.int32),
                     lax.bitcast_convert_type(route_f[1], jnp.int32)], axis=1)
    init = jnp.broadcast_to(jnp.array([n_tok, 0, 0], jnp.int32), (n_blocks * bm, 3))
    packed = init.at[dest].set(upd, unique_indices=True)
    tok_rows = packed[:, 0]
    gate_rows = lax.bitcast_convert_type(packed[:, 1:3], F32)
    blk_bucket = jnp.minimum(
        jnp.searchsorted(blk_end, jnp.arange(n_blocks, dtype=jnp.int32), side='right'),
        N_BUCKETS - 1).astype(jnp.int32)
    nused = blk_end[-1:].astype(jnp.int32)
    last = blk_bucket[jnp.maximum(nused[0] - 1, 0)]
    blk_bucket = jnp.where(jnp.arange(n_blocks) < nused[0], blk_bucket, last)
    grp = blk_bucket // N_PAIRS
    pair = blk_bucket % N_PAIRS
    blk_ea = grp * EXPERTS_PER_GROUP + jnp.asarray(_PAIR_A)[pair]
    blk_eb = grp * EXPERTS_PER_GROUP + jnp.asarray(_PAIR_B)[pair]
    row = jnp.arange(n_blocks * bm, dtype=jnp.int32)
    gather_rows = jnp.minimum(tok_rows, n_tok - 1)
    scatter_rows = jnp.where(tok_rows >= n_tok, n_tok + (row // bm) % 2 * bm + row % bm, tok_rows)
    scatter_rows = jnp.concatenate([n_tok + bm + row[:bm], scatter_rows])
    return _moe_ffn(h, n_tok, nused, blk_ea, blk_eb, gather_rows.reshape(n_blocks, 1, bm),
                    scatter_rows.reshape(n_blocks + 1, 1, bm), gate_rows, nw, layer, wg, wu, wd, fw,
                    final_norm)


def _odd_in_kernel(h_ref, nw_ref, wz_ref, wx_ref, wdt_ref, wdtt_ref, cw_ref, cb_ref, dtb_ref, dtbt_ref,
                   z_ref, x_ref, b_ref, c_ref, dt_ref, dtt_ref, xbs, hist, *, tm, rt, tiles_per_seq):
    i = pl.program_id(0)
    xbs[...] = _rms(h_ref[...], nw_ref[...]).astype(BF16)
    wchunk = 512
    row8 = lax.broadcasted_iota(jnp.int32, (SUBLANES, wchunk), 0)

    @pl.when(i % tiles_per_seq == 0)
    def _():
        hist[...] = jnp.zeros(hist.shape, F32)

    def sub_tile(s, carry):
        rows = pl.ds(pl.multiple_of(s * rt, rt), rt)
        xb = xbs[rows, :]
        for c0 in range(0, D_INNER, wchunk):
            z_ref[rows, c0:c0 + wchunk] = _dot(xb, wz_ref[:, c0:c0 + wchunk]).astype(BF16)
        for c0 in range(0, SSM_CONV_DIM, wchunk):
            cols = slice(c0, c0 + wchunk)
            u = _dot(xb, wx_ref[:, cols])
            prev = hist[:, cols]
            hist[:, cols] = u[rt - SUBLANES:, :]
            u3 = u.reshape(rt // SUBLANES, SUBLANES, wchunk)
            conv = cw_ref[SSM_CONV_K - 1:SSM_CONV_K, cols] * u3 + cb_ref[:, cols]
            for j in range(1, SSM_CONV_K):
                rot = pltpu.roll(u3, j, 1)
                before = jnp.concatenate([pltpu.roll(prev, j, 0)[None], rot[:-1]], axis=0)
                conv = conv + cw_ref[SSM_CONV_K - 1 - j:SSM_CONV_K - j, cols] * jnp.where(row8 < j, before, rot)
            act = jax.nn.silu(conv).reshape(rt, wchunk).astype(BF16)
            if c0 < D_INNER:
                x_ref[rows, c0:c0 + wchunk] = act
            elif c0 < D_INNER + SSM_GN:
                b_ref[rows, c0 - D_INNER:c0 - D_INNER + wchunk] = act
            else:
                c_ref[rows, c0 - D_INNER - SSM_GN:c0 - D_INNER - SSM_GN + wchunk] = act
        return carry

    lax.fori_loop(0, tm // rt, sub_tile, 0)
    xb = xbs[...]
    dt_ref[...] = jax.nn.softplus(_dot(xb, wdt_ref[...]) + dtb_ref[...])
    dtt_ref[...] = jax.nn.softplus(_dot_nt(wdtt_ref[...], xb) + dtbt_ref[...])


def _odd_in_proj(h, n_tok, seq, nw, wz, wx, wdt, wdtt, conv_w, conv_b, dtb, dtbt):
    tm = ROW_TILE
    row = lambda i: (i, 0)
    col = lambda i: (0, i)
    fixed = lambda i: (0, 0)
    full = lambda a: pl.BlockSpec(a.shape, fixed)
    return pl.pallas_call(
        functools.partial(_odd_in_kernel, tm=tm, rt=ODD_SUB_TILE, tiles_per_seq=seq // tm),
        grid=(n_tok // tm,),
        in_specs=[pl.BlockSpec((tm, D_MODEL), row), full(nw), full(wz), full(wx), full(wdt), full(wdtt),
                  full(conv_w), full(conv_b), full(dtb), full(dtbt)],
        out_specs=[pl.BlockSpec((tm, D_INNER), row), pl.BlockSpec((tm, D_INNER), row),
                   pl.BlockSpec((tm, SSM_GN), row), pl.BlockSpec((tm, SSM_GN), row),
                   pl.BlockSpec((tm, SSM_HEADS), row), pl.BlockSpec((SSM_HEADS, tm), col)],
        out_shape=[jax.ShapeDtypeStruct((n_tok, D_INNER), BF16), jax.ShapeDtypeStruct((n_tok, D_INNER), BF16),
                   jax.ShapeDtypeStruct((n_tok, SSM_GN), BF16), jax.ShapeDtypeStruct((n_tok, SSM_GN), BF16),
                   jax.ShapeDtypeStruct((n_tok, SSM_HEADS), F32), jax.ShapeDtypeStruct((SSM_HEADS, n_tok), F32)],
        scratch_shapes=[pltpu.VMEM((tm, D_MODEL), BF16), pltpu.VMEM((SUBLANES, SSM_CONV_DIM), F32)],
        compiler_params=_params("arbitrary"),
        name="odd_in_proj",
    )(h, nw, wz, wx, wdt, wdtt, conv_w, conv_b, dtb, dtbt)


def _ssd_kernel(x_ref, b_ref, c_ref, z_ref, dt_ref, dtt_ref, alog_ref, alogt_ref, dskip_ref, normw_ref,
                tril_ref, triu_ref, spread_ref, y_ref, state, *, lc):
    ci = pl.program_id(1)

    @pl.when(ci == 0)
    def _():
        state[...] = jnp.zeros(state.shape, F32)

    gw = SSM_HPG * SSM_HEAD_DIM
    hi = lax.Precision.HIGHEST
    dt = dt_ref[...]
    dtt = dtt_ref[...]
    a = -jnp.exp(alog_ref[...])
    at = -jnp.exp(alogt_ref[...])
    cs = jnp.dot(tril_ref[...], dt * a, precision=hi, preferred_element_type=F32)
    cst = jnp.dot(dtt * at, triu_ref[...], precision=hi, preferred_element_type=F32)
    cs_last = cs[lc - 1:lc, :]
    grow = jnp.exp(cs)
    dt_end = dt * jnp.exp(cs_last - cs)
    e_last = jnp.exp(cs_last)
    rr = lax.broadcasted_iota(jnp.int32, (lc, lc), 0)
    cc = lax.broadcasted_iota(jnp.int32, (lc, lc), 1)
    lower = cc <= rr
    lane = lax.broadcasted_iota(jnp.int32, (lc, gw), 1)
    head_of_lane = lane // SSM_HEAD_DIM
    fac = jnp.concatenate([dt, grow, dt_end, jnp.broadcast_to(e_last, (SUBLANES, SSM_HEADS))], axis=0)
    fac_hi = fac.astype(BF16).astype(F32)
    spread = _dot(jnp.concatenate([fac_hi, fac - fac_hi], axis=1).astype(BF16), spread_ref[...])

    for g in range(SSM_GROUPS):
        h0 = g * SSM_HPG
        bg = b_ref[:, g * SSM_STATE:(g + 1) * SSM_STATE]
        cg = c_ref[:, g * SSM_STATE:(g + 1) * SSM_STATE]
        xg = x_ref[:, g * gw:(g + 1) * gw].astype(F32)
        cb = _dot_nt(cg, bg)
        cols = slice(g * gw, (g + 1) * gw)
        xdt = xg * spread[0:lc, cols]
        xdt_b = xdt.astype(BF16)
        st = state[g]
        y = _dot(cg, st.astype(BF16)) * spread[lc:2 * lc, cols]
        for k in range(SSM_HPG):
            h = h0 + k
            seg = cs[:, h:h + 1] - cst[h:h + 1, :]
            decay = jnp.exp(jnp.where(lower, seg, -jnp.inf))
            m = (cb * decay).astype(BF16)
            y = y + _dot(m, jnp.where(head_of_lane == k, xdt_b, jnp.zeros_like(xdt_b)))
        x_end = (xg * spread[2 * lc:3 * lc, cols]).astype(BF16)
        bgt = bg.astype(F32).T.astype(BF16)
        state[g] = st * spread[3 * lc:3 * lc + 1, cols] + _dot(bgt, x_end)
        y = y + dskip_ref[:, g * gw:(g + 1) * gw] * xg
        y = y * jax.nn.silu(z_ref[:, g * gw:(g + 1) * gw].astype(F32))
        y = y * lax.rsqrt(jnp.mean(y * y, axis=-1, keepdims=True) + RMS_EPS)
        y_ref[:, g * gw:(g + 1) * gw] = (y * normw_ref[:, g * gw:(g + 1) * gw]).astype(BF16)


def _ssd(x, bm, cm, z, dt, dtt, n_tok, seq, alog, alogt, dskip, normw):
    lc = SSD_CHUNK
    nc = seq // lc
    tril = jnp.asarray(np.tril(np.ones((lc, lc), np.float32)))
    triu = jnp.asarray(np.triu(np.ones((lc, lc), np.float32)))
    head_of_col = np.arange(D_INNER) // SSM_HEAD_DIM
    spread = jnp.asarray(np.arange(2 * SSM_HEADS)[:, None] % SSM_HEADS == head_of_col[None, :], BF16)
    row = lambda b, c: (b * nc + c, 0)
    col = lambda b, c: (0, b * nc + c)
    fixed = lambda b, c: (0, 0)
    full = lambda a: pl.BlockSpec(a.shape, fixed)
    return pl.pallas_call(
        functools.partial(_ssd_kernel, lc=lc),
        grid=(n_tok // seq, nc),
        in_specs=[pl.BlockSpec((lc, D_INNER), row), pl.BlockSpec((lc, SSM_GN), row),
                  pl.BlockSpec((lc, SSM_GN), row), pl.BlockSpec((lc, D_INNER), row),
                  pl.BlockSpec((lc, SSM_HEADS), row), pl.BlockSpec((SSM_HEADS, lc), col),
                  full(alog), full(alogt), full(dskip), full(normw), full(tril), full(triu),
                  full(spread)],
        out_specs=pl.BlockSpec((lc, D_INNER), row),
        out_shape=jax.ShapeDtypeStruct((n_tok, D_INNER), BF16),
        scratch_shapes=[pltpu.VMEM((SSM_GROUPS, SSM_STATE, SSM_HPG * SSM_HEAD_DIM), F32)],
        compiler_params=_params("arbitrary", "arbitrary"),
        name="ssd_scan",
    )(x, bm, cm, z, dt, dtt, alog, alogt, dskip, normw, tril, triu, spread)


def _router_weights(wg, bg, we, be):
    w = jnp.concatenate([wg, jnp.moveaxis(we, 0, 1).reshape(D_MODEL, -1)], axis=1)
    b = jnp.concatenate([bg, be.reshape(-1)])
    pad = LANES - w.shape[1]
    w = jnp.pad(w, ((0, 0), (0, pad)))
    b = jnp.pad(b, (0, pad)).reshape(1, LANES)
    w_hi = w.astype(BF16)
    w_lo = (w - w_hi.astype(F32)).astype(BF16)
    return w_hi, w_lo, b


def kernel(x, even_w_in, even_conv_w, even_w_out, odd_in_proj, odd_conv_w, odd_conv_b, odd_dt_bias,
           odd_a_log, odd_d_skip, odd_norm_w, odd_out_proj, norm_mix, norm_ffn, router_g_w, router_g_b,
           router_e_w, router_e_b, expert_w_gate, expert_w_up, expert_w_down, final_norm):
    bsz, seq, d = x.shape
    n_tok = bsz * seq
    depth = norm_mix.shape[0]
    h = x.reshape(n_tok, d)
    fw = final_norm.reshape(1, d)
    for layer in range(depth):
        i = layer // 2
        nw = norm_mix[layer].reshape(1, d)
        if layer % 2 == 0:
            conv, q, k, v = _even_in_proj(h, n_tok, seq, nw, even_w_in[i].astype(BF16), even_conv_w[i])
            sb = _stick_breaking(q, k, v, n_tok, seq)
            w_out = even_w_out[i].astype(BF16)
            acts, ws = [conv, sb], [w_out[:CONV_WIDTH], w_out[CONV_WIDTH:]]
        else:
            w_in = odd_in_proj[i]
            wz = w_in[:, :D_INNER].astype(BF16)
            wx = w_in[:, D_INNER:D_INNER + SSM_CONV_DIM].astype(BF16)
            wdt = w_in[:, D_INNER + SSM_CONV_DIM:].astype(BF16)
            z, xs, bmat, cmat, dt, dtt = _odd_in_proj(
                h, n_tok, seq, nw, wz, wx, wdt, wdt.T, odd_conv_w[i], odd_conv_b[i].reshape(1, -1),
                odd_dt_bias[i].reshape(1, -1), odd_dt_bias[i].reshape(-1, 1))
            y = _ssd(xs, bmat, cmat, z, dt, dtt, n_tok, seq, odd_a_log[i].reshape(1, -1),
                     odd_a_log[i].reshape(-1, 1), jnp.repeat(odd_d_skip[i], SSM_HEAD_DIM).reshape(1, -1),
                     odd_norm_w[i].reshape(1, -1))
            acts, ws = [y], [odd_out_proj[i].astype(BF16)]
        wr_hi, wr_lo, br = _router_weights(router_g_w[layer], router_g_b[layer], router_e_w[layer],
                                           router_e_b[layer])
        h, route_i, route_f, counts = _mix_out_router(h, acts, ws, n_tok, norm_ffn[layer].reshape(1, d),
                                                      wr_hi, wr_lo, br)
        h = _moe_layer(h, n_tok, route_i, route_f, counts, norm_ffn[layer].reshape(1, d),
                       layer, expert_w_gate, expert_w_up, expert_w_down, fw, final_norm=(layer == depth - 1))
    return h[:n_tok].reshape(bsz, seq, d)
```

```python
import functools

import numpy as np
import jax
import jax.numpy as jnp
from jax import lax
from jax.experimental import pallas as pl
from jax.experimental.pallas import tpu as pltpu

F32 = jnp.float32
BF16 = jnp.bfloat16

D_MODEL = 1024
RMS_EPS = 1e-6
CONV_WIDTH = 512
SHORT_CONV_K = 3
SB_HEAD_DIM = 64
SB_WIDTH = 512
D_INNER = 2048
SSM_HEAD_DIM = 64
SSM_HEADS = 32
SSM_GROUPS = 8
SSM_HPG = 4
SSM_STATE = 128
SSM_CONV_K = 4
SSM_GN = SSM_GROUPS * SSM_STATE
SSM_CONV_DIM = D_INNER + 2 * SSM_GN
N_EXPERT_GROUPS = 4
EXPERTS_PER_GROUP = 8
EXPERT_FF = 512
N_PAIRS = EXPERTS_PER_GROUP * (EXPERTS_PER_GROUP - 1) // 2
N_BUCKETS = N_EXPERT_GROUPS * N_PAIRS

LANES = 128
SUBLANES = 8
N_LANE_TILES = D_MODEL // LANES
VMEM_LIMIT = 56 * 1024 * 1024

ROW_TILE = 512
SB_BLOCK = 256
SSD_CHUNK = 128
MOE_BLOCK = 256


def _params(*sem):
    return pltpu.CompilerParams(dimension_semantics=sem, vmem_limit_bytes=VMEM_LIMIT)


def _rms(x, w):
    return x * lax.rsqrt(jnp.mean(x * x, axis=-1, keepdims=True) + RMS_EPS) * w


def _dot(a, b):
    return jnp.dot(a, b, preferred_element_type=F32)


def _load_rows(ref):
    if ref.shape[1] == D_MODEL:
        return ref[...]
    n = ref.shape[0] // N_LANE_TILES
    return jnp.concatenate([ref[pl.ds(s, n, stride=N_LANE_TILES), :] for s in range(N_LANE_TILES)], axis=1)


def _store_rows(ref, val):
    if ref.shape[1] == D_MODEL:
        ref[...] = val
    else:
        n = ref.shape[0] // N_LANE_TILES
        for s in range(N_LANE_TILES):
            ref[pl.ds(s, n, stride=N_LANE_TILES), :] = val[:, s * LANES:(s + 1) * LANES]


def _rows_spec(h, tm):
    if h.shape[1] == D_MODEL:
        return pl.BlockSpec((tm, D_MODEL), lambda i: (i, 0))
    return pl.BlockSpec((tm * N_LANE_TILES, LANES), lambda i: (i, 0))


def _dot_nt(a, b):
    return lax.dot_general(a, b, (((1,), (1,)), ((), ())), preferred_element_type=F32)


def _even_in_kernel(h_ref, nw_ref, w_ref, cw_ref, conv_ref, q_ref, k_ref, v_ref, ubuf,
                    *, tm, tiles_per_seq):
    i = pl.program_id(0)
    xb = _rms(_load_rows(h_ref), nw_ref[...]).astype(BF16)
    c = CONV_WIDTH

    def proj(c0):
        return _dot(xb, w_ref[:, c0:c0 + c])

    gate_b = proj(0)
    u = proj(c) * proj(2 * c)

    @pl.when(i % tiles_per_seq == 0)
    def _():
        ubuf[0:SUBLANES, :] = jnp.zeros((SUBLANES, c), F32)

    ubuf[SUBLANES:SUBLANES + tm, :] = u
    cw = cw_ref[...]
    conv = cw[2:3, :] * u + cw[1:2, :] * ubuf[SUBLANES - 1:SUBLANES - 1 + tm, :] \
        + cw[0:1, :] * ubuf[SUBLANES - 2:SUBLANES - 2 + tm, :]
    conv_ref[...] = (gate_b * conv).astype(BF16)
    ubuf[0:SUBLANES, :] = ubuf[tm:tm + SUBLANES, :]
    q_ref[...] = (proj(3 * c) * (SB_HEAD_DIM ** -0.5)).astype(BF16)
    k_ref[...] = proj(4 * c).astype(BF16)
    v_ref[...] = proj(5 * c).astype(BF16)


def _even_in_proj(h, n_tok, seq, nw, w_in, conv_w):
    tm = ROW_TILE
    c = CONV_WIDTH
    row = lambda i: (i, 0)
    fixed = lambda i: (0, 0)
    out = jax.ShapeDtypeStruct((n_tok, c), BF16)
    return pl.pallas_call(
        functools.partial(_even_in_kernel, tm=tm, tiles_per_seq=seq // tm),
        grid=(n_tok // tm,),
        in_specs=[_rows_spec(h, tm), pl.BlockSpec((1, D_MODEL), fixed),
                  pl.BlockSpec((D_MODEL, 6 * c), fixed), pl.BlockSpec((SHORT_CONV_K, c), fixed)],
        out_specs=[pl.BlockSpec((tm, c), row)] * 4,
        out_shape=[out] * 4,
        scratch_shapes=[pltpu.VMEM((tm + SUBLANES, c), F32)],
        compiler_params=_params("arbitrary"),
        name="even_in_proj",
    )(h, nw, w_in, conv_w)


def _sb_kernel(q_ref, k_ref, v_ref, tri_ref, o_ref, *, seq, bq):
    nq = seq // bq
    lane = lax.broadcasted_iota(jnp.int32, (bq, LANES), 1)
    rr = lax.broadcasted_iota(jnp.int32, (bq, bq), 0)
    cc = lax.broadcasted_iota(jnp.int32, (bq, bq), 1)
    causal = cc < rr
    every = cc >= 0
    tri = tri_ref[...]

    def block(qs, kb, vb, run, acc, mask):
        z = _dot_nt(qs, kb)
        sp = jnp.maximum(z, 0.0) + jnp.log(1.0 + jnp.exp(-jnp.abs(z)))
        spm = sp if mask is None else jnp.where(mask, sp, 0.0)
        suffix = _dot(spm.astype(BF16), tri) + run
        p = jnp.exp((z - sp) + suffix)
        if mask is not None:
            p = jnp.where(mask, p, 0.0)
        acc = acc + _dot(p.astype(BF16), vb)
        run = run - jnp.sum(spm, axis=1, keepdims=True)
        return run, acc

    def kv(kj):
        rows = pl.ds(pl.multiple_of(kj * bq, bq), bq)
        return k_ref[rows, :], v_ref[rows, :]

    def heads(qi):
        q = q_ref[qi * bq:(qi + 1) * bq, :]
        return [jnp.where((lane < SB_HEAD_DIM) == (head == 0), q, jnp.zeros_like(q)) for head in range(2)]

    def store(qi, acc):
        o_ref[qi * bq:(qi + 1) * bq, :] = jnp.where(lane < SB_HEAD_DIM, acc[:bq], acc[bq:]).astype(BF16)

    for qa in range(0, nq, 2):
        qb = qa + 1
        qs_b = jnp.concatenate(heads(qb), axis=0)
        qs = jnp.concatenate(heads(qa) + [qs_b], axis=0)
        kb, vb = kv(qb)
        run_b, acc_b = block(qs_b, kb, vb, jnp.zeros((2 * bq, 1), F32), jnp.zeros((2 * bq, LANES), F32),
                             jnp.concatenate([causal, causal], axis=0))
        kb, vb = kv(qa)
        run = jnp.concatenate([jnp.zeros((2 * bq, 1), F32), run_b], axis=0)
        acc = jnp.concatenate([jnp.zeros((2 * bq, LANES), F32), acc_b], axis=0)
        run, acc = block(qs, kb, vb, run, acc, jnp.concatenate([causal, causal, every, every], axis=0))
        if qa > 0:
            def body(it, cr, qs=qs, qa=qa):
                kb, vb = kv(qa - 1 - it)
                return block(qs, kb, vb, cr[0], cr[1], None)
            run, acc = lax.fori_loop(0, qa, body, (run, acc))
        store(qa, acc[:2 * bq])
        store(qb, acc[2 * bq:])


def _stick_breaking(q, k, v, n_tok, seq):
    bq = SB_BLOCK
    tri = jnp.asarray(-np.tril(np.ones((bq, bq), np.float32), -1), BF16)
    blk = pl.BlockSpec((seq, LANES), lambda b, hp: (b, hp))
    return pl.pallas_call(
        functools.partial(_sb_kernel, seq=seq, bq=bq),
        grid=(n_tok // seq, SB_WIDTH // LANES),
        in_specs=[blk, blk, blk, pl.BlockSpec((bq, bq), lambda b, hp: (0, 0))],
        out_specs=blk,
        out_shape=jax.ShapeDtypeStruct((n_tok, SB_WIDTH), BF16),
        compiler_params=_params("arbitrary", "arbitrary"),
        name="stick_breaking",
    )(q, k, v, tri)


def _mix_out_kernel(*refs, n_act, tm):
    h_ref = refs[0]
    act_refs = refs[1:1 + n_act]
    w_refs = refs[1 + n_act:1 + 2 * n_act]
    nw_ref, wrh_ref, wrl_ref, br_ref, triu_ref = refs[1 + 2 * n_act:6 + 2 * n_act]
    hout_ref, ri_ref, rf_ref, cnt_ref, carry = refs[6 + 2 * n_act:]
    i = pl.program_id(0)

    @pl.when(i == 0)
    def _():
        carry[...] = jnp.zeros(carry.shape, F32)

    acc = _load_rows(h_ref)
    for a_ref, w_ref in zip(act_refs, w_refs):
        acc = acc + _dot(a_ref[...], w_ref[...])
    _store_rows(hout_ref, acc)

    xn = _rms(acc, nw_ref[...])
    xh = xn.astype(BF16)
    xl = (xn - xh.astype(F32)).astype(BF16)
    wrh = wrh_ref[...]
    logits = _dot(xh, wrh) + _dot(xl, wrh) + _dot(xh, wrl_ref[...]) + br_ref[...]
    lt = logits.T
    r = lax.broadcasted_iota(jnp.int32, (LANES, tm), 0).astype(F32)
    big = float(LANES)
    neg = -jnp.inf

    def first_argmax(vals):
        m = jnp.max(vals, axis=0, keepdims=True)
        return m, jnp.min(jnp.where(vals == m, r, big), axis=0, keepdims=True)

    gl = jnp.where(r < N_EXPERT_GROUPS, lt, neg)
    gmax, gidx = first_argmax(gl)
    g_top_p = 1.0 / jnp.sum(jnp.exp(gl - gmax), axis=0, keepdims=True)
    lo = N_EXPERT_GROUPS + EXPERTS_PER_GROUP * gidx
    el = jnp.where((r >= lo) & (r < lo + EXPERTS_PER_GROUP), lt, neg)
    m1, i1 = first_argmax(el)
    m2, i2 = first_argmax(jnp.where(r == i1, neg, el))
    t = jnp.exp(m2 - m1)
    g1 = g_top_p / (1.0 + t)
    g2 = g_top_p * t / (1.0 + t)
    e1 = i1 - lo
    e2 = i2 - lo
    ea = jnp.minimum(e1, e2)
    eb = jnp.maximum(e1, e2)
    pair = ea * (2 * EXPERTS_PER_GROUP - 1 - ea) * 0.5 + (eb - ea - 1.0)
    bucket = gidx * N_PAIRS + pair
    first_is_a = e1 < e2
    gate_a = jnp.where(first_is_a, g1, g2)
    gate_b = jnp.where(first_is_a, g2, g1)

    onehot = r == bucket
    cum = _dot(jnp.where(onehot, 1.0, 0.0).astype(BF16), triu_ref[...])
    prev = carry[...]
    rank = jnp.sum(jnp.where(onehot, cum + prev[:, 0:1], 0.0), axis=0, keepdims=True)
    new = prev + jnp.sum(jnp.where(onehot, 1.0, 0.0), axis=1, keepdims=True)
    carry[...] = new
    cnt_ref[...] = new

    row8 = lax.broadcasted_iota(jnp.int32, (SUBLANES, tm), 0)
    ri_ref[...] = jnp.where(row8 == 0, bucket, jnp.where(row8 == 1, rank, 0.0)).astype(jnp.int32)
    rf_ref[...] = jnp.where(row8 == 0, gate_a, jnp.where(row8 == 1, gate_b, 0.0))


def _mix_out_router(h, acts, ws, n_tok, nw, wr_hi, wr_lo, br):
    tm = ROW_TILE
    row = lambda i: (i, 0)
    col = lambda i: (0, i)
    fixed = lambda i: (0, 0)
    triu = jnp.asarray(np.triu(np.ones((tm, tm), np.float32), 1), BF16)
    in_specs = [_rows_spec(h, tm)]
    in_specs += [pl.BlockSpec((tm, a.shape[1]), row) for a in acts]
    in_specs += [pl.BlockSpec(w.shape, fixed) for w in ws]
    in_specs += [pl.BlockSpec((1, D_MODEL), fixed), pl.BlockSpec((D_MODEL, LANES), fixed),
                 pl.BlockSpec((D_MODEL, LANES), fixed), pl.BlockSpec((1, LANES), fixed),
                 pl.BlockSpec((tm, tm), fixed)]
    return pl.pallas_call(
        functools.partial(_mix_out_kernel, n_act=len(acts), tm=tm),
        grid=(n_tok // tm,),
        in_specs=in_specs,
        out_specs=[pl.BlockSpec((tm * N_LANE_TILES, LANES), row), pl.BlockSpec((SUBLANES, tm), col),
                   pl.BlockSpec((SUBLANES, tm), col), pl.BlockSpec((LANES, LANES), fixed)],
        out_shape=[jax.ShapeDtypeStruct((n_tok * N_LANE_TILES, LANES), F32),
                   jax.ShapeDtypeStruct((SUBLANES, n_tok), jnp.int32),
                   jax.ShapeDtypeStruct((SUBLANES, n_tok), F32),
                   jax.ShapeDtypeStruct((LANES, LANES), F32)],
        scratch_shapes=[pltpu.VMEM((LANES, LANES), F32)],
        compiler_params=_params("arbitrary"),
        name="mix_out_router",
    )(h, *acts, *ws, nw, wr_hi, wr_lo, br, triu)


def _ffn_kernel(nused_ref, ea_ref, eb_ref, gcur_ref, gnext_ref, sprev_ref, g_ref, nw_ref,
                wga_ref, wua_ref, wda_ref, wgb_ref, wub_ref, wdb_ref, fw_ref, h_hbm,
                out_hbm, xbuf0, xbuf1, obuf0, obuf1, gsem, ssem, *, bm, n_tok, rpo, final_norm):
    del ea_ref, eb_ref
    i = pl.program_id(0)
    slot = i % 2
    nused = nused_ref[0]
    xbufs = (xbuf0, xbuf1)
    obufs = (obuf0, obuf1)

    rpi = N_LANE_TILES

    def tok_rows(t, n):
        return pl.ds(t * n if isinstance(t, int) else pl.multiple_of(t * n, n), n)

    def row_in(gref, s, r):
        return pltpu.make_async_copy(h_hbm.at[tok_rows(gref[0, 0, r], rpi)], xbufs[s].at[tok_rows(r, rpi)],
                                     gsem.at[s])

    def row_out(s, r):
        return pltpu.make_async_copy(obufs[s].at[tok_rows(r, rpo)], out_hbm.at[tok_rows(sprev_ref[0, 0, r], rpo)],
                                     ssem.at[s])

    def wait_gather(s):
        pltpu.make_async_copy(h_hbm.at[pl.ds(0, bm * rpi)], xbufs[s], gsem.at[s]).wait()

    def wait_scatter(s):
        pltpu.make_async_copy(obufs[s], out_hbm.at[pl.ds(0, bm * rpo)], ssem.at[s]).wait()

    @pl.when(i == 0)
    def _():
        def body(r, c):
            row_in(gcur_ref, 0, r).start()
            return c
        lax.fori_loop(0, bm, body, 0, unroll=8)
        for s in range(2):
            obufs[s][...] = jnp.zeros(obufs[s].shape, F32)
            cp = pltpu.make_async_copy(obufs[s], out_hbm.at[pl.ds((n_tok + s * bm) * rpo, bm * rpo)], ssem.at[s])
            cp.start()
            if s == 1:
                cp.wait()

    def run_block(cur, nxt):
        for r in range(bm):
            row_in(gnext_ref, nxt, r).start()
        wait_gather(cur)
        x = _load_rows(xbufs[cur])
        xb = _rms(x, nw_ref[...]).astype(BF16)
        g = g_ref[...]

        def hidden(wg_ref, wu_ref, gate):
            return (jax.nn.silu(_dot(xb, wg_ref[0, 0].astype(BF16))) * _dot(xb, wu_ref[0, 0].astype(BF16))
                    * gate).astype(BF16)

        hid_a = hidden(wga_ref, wua_ref, g[:, 0:1])
        hid_b = hidden(wgb_ref, wub_ref, g[:, 1:2])
        for r in range(bm):
            row_out(nxt, r).start()
        wait_scatter(cur)
        y = x + _dot(hid_a, wda_ref[0, 0].astype(BF16)) + _dot(hid_b, wdb_ref[0, 0].astype(BF16))
        if final_norm:
            y = _rms(y, fw_ref[...])
        _store_rows(obufs[cur], y)

    for s in range(2):
        @pl.when((i < nused) & (slot == s))
        def _(s=s):
            run_block(s, 1 - s)

    @pl.when(i == nused)
    def _():
        for s in range(2):
            @pl.when(slot == s)
            def _(s=s):
                wait_gather(s)
                wait_scatter(s)

                def body(r, c):
                    row_out(1 - s, r).start()
                    return c
                lax.fori_loop(0, bm, body, 0, unroll=8)
                wait_scatter(1 - s)


def _moe_ffn(h, n_tok, nused, blk_ea, blk_eb, gather_rows, scatter_rows, gate_rows, nw, layer, wg, wu, wd, fw,
             final_norm):
    bm = MOE_BLOCK
    n_blocks = gather_rows.shape[0]
    rpo = 1 if final_norm else N_LANE_TILES
    out_cols = D_MODEL // rpo
    cur = lambda i, nu, ea, eb: (i, 0, 0)
    nxt = lambda i, nu, ea, eb: (jnp.minimum(i + 1, n_blocks - 1), 0, 0)
    idx_spec = lambda imap: pl.BlockSpec((1, 1, bm), imap, memory_space=pltpu.SMEM)
    fixed = lambda i, nu, ea, eb: (0, 0)
    wa = lambda i, nu, ea, eb: (layer, ea[i], 0, 0)
    wb = lambda i, nu, ea, eb: (layer, eb[i], 0, 0)
    up = pl.BlockSpec((1, 1, D_MODEL, EXPERT_FF), wa)
    upb = pl.BlockSpec((1, 1, D_MODEL, EXPERT_FF), wb)
    grid_spec = pltpu.PrefetchScalarGridSpec(
        num_scalar_prefetch=3,
        grid=(n_blocks,),
        in_specs=[idx_spec(cur), idx_spec(nxt), idx_spec(cur),
                  pl.BlockSpec((bm, 2), lambda i, nu, ea, eb: (i, 0)),
                  pl.BlockSpec((1, D_MODEL), fixed),
                  up, up, pl.BlockSpec((1, 1, EXPERT_FF, D_MODEL), wa),
                  upb, upb, pl.BlockSpec((1, 1, EXPERT_FF, D_MODEL), wb),
                  pl.BlockSpec((1, D_MODEL), fixed),
                  pl.BlockSpec(memory_space=pl.ANY)],
        out_specs=pl.BlockSpec(memory_space=pl.ANY),
        scratch_shapes=[pltpu.VMEM((bm * N_LANE_TILES, LANES), F32)] * 2 + [pltpu.VMEM((bm * rpo, out_cols), F32)] * 2
        + [pltpu.SemaphoreType.DMA((2,)), pltpu.SemaphoreType.DMA((2,))],
    )
    return pl.pallas_call(
        functools.partial(_ffn_kernel, bm=bm, n_tok=n_tok, rpo=rpo, final_norm=final_norm),
        grid_spec=grid_spec,
        out_shape=jax.ShapeDtypeStruct(((n_tok + 2 * bm) * rpo, out_cols), F32),
        compiler_params=_params("arbitrary"),
        name="moe_ffn",
    )(nused, blk_ea, blk_eb, gather_rows, gather_rows, scatter_rows, gate_rows, nw, wg, wu, wd, wg, wu, wd,
      fw, h)


_PAIR_A = np.array([a for a in range(EXPERTS_PER_GROUP) for b in range(a + 1, EXPERTS_PER_GROUP)], np.int32)
_PAIR_B = np.array([b for a in range(EXPERTS_PER_GROUP) for b in range(a + 1, EXPERTS_PER_GROUP)], np.int32)


def _moe_layer(h, n_tok, route_i, route_f, counts, nw, layer, wg, wu, wd, fw, final_norm):
    bm = MOE_BLOCK
    n_blocks = n_tok // bm + N_BUCKETS + 1
    bucket = route_i[0]
    rank = route_i[1]
    cnt = counts[:N_BUCKETS, 0].astype(jnp.int32)
    nblk = (cnt + bm - 1) // bm
    blk_end = jnp.cumsum(nblk)
    blk_start = blk_end - nblk
    in_bucket = bucket[:, None] == jnp.arange(N_BUCKETS, dtype=jnp.int32)[None, :]
    dest = jnp.sum(jnp.where(in_bucket, blk_start[None, :], 0), axis=1) * bm + rank
    upd = jnp.stack([jnp.arange(n_tok, dtype=jnp.int32), lax.bitcast_convert_type(route_f[0], jnp.int32),
                     lax.bitcast_convert_type(route_f[1], jnp.int32)], axis=1)
    init = jnp.broadcast_to(jnp.array([n_tok, 0, 0], jnp.int32), (n_blocks * bm, 3))
    packed = init.at[dest].set(upd, unique_indices=True)
    tok_rows = packed[:, 0]
    gate_rows = lax.bitcast_convert_type(packed[:, 1:3], F32)
    blk_bucket = jnp.minimum(
        jnp.searchsorted(blk_end, jnp.arange(n_blocks, dtype=jnp.int32), side='right'),
        N_BUCKETS - 1).astype(jnp.int32)
    nused = blk_end[-1:].astype(jnp.int32)
    last = blk_bucket[jnp.maximum(nused[0] - 1, 0)]
    blk_bucket = jnp.where(jnp.arange(n_blocks) < nused[0], blk_bucket, last)
    grp = blk_bucket // N_PAIRS
    pair = blk_bucket % N_PAIRS
    blk_ea = grp * EXPERTS_PER_GROUP + jnp.asarray(_PAIR_A)[pair]
    blk_eb = grp * EXPERTS_PER_GROUP + jnp.asarray(_PAIR_B)[pair]
    row = jnp.arange(n_blocks * bm, dtype=jnp.int32)
    gather_rows = jnp.minimum(tok_rows, n_tok - 1)
    scatter_rows = jnp.where(tok_rows >= n_tok, n_tok + (row // bm) % 2 * bm + row % bm, tok_rows)
    scatter_rows = jnp.concatenate([n_tok + bm + row[:bm], scatter_rows])
    return _moe_ffn(h, n_tok, nused, blk_ea, blk_eb, gather_rows.reshape(n_blocks, 1, bm),
                    scatter_rows.reshape(n_blocks + 1, 1, bm), gate_rows, nw, layer, wg, wu, wd, fw,
                    final_norm)


def _odd_in_kernel(h_ref, nw_ref, wz_ref, wx_ref, wdt_ref, wdtt_ref, cw_ref, cb_ref, dtb_ref, dtbt_ref,
                   z_ref, x_ref, b_ref, c_ref, dt_ref, dtt_ref, ubuf, *, tm, tiles_per_seq):
    i = pl.program_id(0)
    xb = _rms(_load_rows(h_ref), nw_ref[...]).astype(BF16)
    cw_all = cw_ref[...]
    cb_all = cb_ref[...]
    wchunk = 512

    @pl.when(i % tiles_per_seq == 0)
    def _():
        ubuf[0:SUBLANES, :] = jnp.zeros((SUBLANES, SSM_CONV_DIM), F32)

    for c0 in range(0, D_INNER, wchunk):
        z_ref[:, c0:c0 + wchunk] = _dot(xb, wz_ref[:, c0:c0 + wchunk]).astype(BF16)

    for c0 in range(0, SSM_CONV_DIM, wchunk):
        cols = slice(c0, c0 + wchunk)
        u = _dot(xb, wx_ref[:, cols])
        ubuf[SUBLANES:SUBLANES + tm, cols] = u
        cw = cw_all[:, cols]
        conv = cw[3:4, :] * u + cb_all[:, cols]
        for j in range(1, SSM_CONV_K):
            conv = conv + cw[3 - j:4 - j, :] * ubuf[SUBLANES - j:SUBLANES - j + tm, cols]
        act = jax.nn.silu(conv).astype(BF16)
        if c0 < D_INNER:
            x_ref[:, c0:c0 + wchunk] = act
        elif c0 < D_INNER + SSM_GN:
            b_ref[:, c0 - D_INNER:c0 - D_INNER + wchunk] = act
        else:
            c_ref[:, c0 - D_INNER - SSM_GN:c0 - D_INNER - SSM_GN + wchunk] = act
    ubuf[0:SUBLANES, :] = ubuf[tm:tm + SUBLANES, :]

    dt_ref[...] = jax.nn.softplus(_dot(xb, wdt_ref[...]) + dtb_ref[...])
    dtt_ref[...] = jax.nn.softplus(_dot_nt(wdtt_ref[...], xb) + dtbt_ref[...])


def _odd_in_proj(h, n_tok, seq, nw, wz, wx, wdt, wdtt, conv_w, conv_b, dtb, dtbt):
    tm = ROW_TILE
    row = lambda i: (i, 0)
    col = lambda i: (0, i)
    fixed = lambda i: (0, 0)
    full = lambda a: pl.BlockSpec(a.shape, fixed)
    return pl.pallas_call(
        functools.partial(_odd_in_kernel, tm=tm, tiles_per_seq=seq // tm),
        grid=(n_tok // tm,),
        in_specs=[_rows_spec(h, tm), full(nw), full(wz), full(wx), full(wdt), full(wdtt),
                  full(conv_w), full(conv_b), full(dtb), full(dtbt)],
        out_specs=[pl.BlockSpec((tm, D_INNER), row), pl.BlockSpec((tm, D_INNER), row),
                   pl.BlockSpec((tm, SSM_GN), row), pl.BlockSpec((tm, SSM_GN), row),
                   pl.BlockSpec((tm, SSM_HEADS), row), pl.BlockSpec((SSM_HEADS, tm), col)],
        out_shape=[jax.ShapeDtypeStruct((n_tok, D_INNER), BF16), jax.ShapeDtypeStruct((n_tok, D_INNER), BF16),
                   jax.ShapeDtypeStruct((n_tok, SSM_GN), BF16), jax.ShapeDtypeStruct((n_tok, SSM_GN), BF16),
                   jax.ShapeDtypeStruct((n_tok, SSM_HEADS), F32), jax.ShapeDtypeStruct((SSM_HEADS, n_tok), F32)],
        scratch_shapes=[pltpu.VMEM((tm + SUBLANES, SSM_CONV_DIM), F32)],
        compiler_params=_params("arbitrary"),
        name="odd_in_proj",
    )(h, nw, wz, wx, wdt, wdtt, conv_w, conv_b, dtb, dtbt)


def _ssd_kernel(x_ref, b_ref, c_ref, z_ref, dt_ref, dtt_ref, alog_ref, alogt_ref, dskip_ref, normw_ref,
                tril_ref, triu_ref, spread_ref, y_ref, state, *, lc):
    ci = pl.program_id(1)

    @pl.when(ci == 0)
    def _():
        state[...] = jnp.zeros(state.shape, F32)

    gw = SSM_HPG * SSM_HEAD_DIM
    hi = lax.Precision.HIGHEST
    dt = dt_ref[...]
    dtt = dtt_ref[...]
    a = -jnp.exp(alog_ref[...])
    at = -jnp.exp(alogt_ref[...])
    cs = jnp.dot(tril_ref[...], dt * a, precision=hi, preferred_element_type=F32)
    cst = jnp.dot(dtt * at, triu_ref[...], precision=hi, preferred_element_type=F32)
    cs_last = cs[lc - 1:lc, :]
    grow = jnp.exp(cs)
    dt_end = dt * jnp.exp(cs_last - cs)
    e_last = jnp.exp(cs_last)
    rr = lax.broadcasted_iota(jnp.int32, (lc, lc), 0)
    cc = lax.broadcasted_iota(jnp.int32, (lc, lc), 1)
    lower = cc <= rr
    lane = lax.broadcasted_iota(jnp.int32, (lc, gw), 1)
    head_of_lane = lane // SSM_HEAD_DIM
    fac = jnp.concatenate([dt, grow, dt_end, jnp.broadcast_to(e_last, (SUBLANES, SSM_HEADS))], axis=0)
    fac_hi = fac.astype(BF16).astype(F32)
    spread = _dot(jnp.concatenate([fac_hi, fac - fac_hi], axis=1).astype(BF16), spread_ref[...])

    for g in range(SSM_GROUPS):
        h0 = g * SSM_HPG
        bg = b_ref[:, g * SSM_STATE:(g + 1) * SSM_STATE]
        cg = c_ref[:, g * SSM_STATE:(g + 1) * SSM_STATE]
        xg = x_ref[:, g * gw:(g + 1) * gw].astype(F32)
        cb = _dot_nt(cg, bg)
        cols = slice(g * gw, (g + 1) * gw)
        xdt = xg * spread[0:lc, cols]
        xdt_b = xdt.astype(BF16)
        st = state[g]
        y = _dot(cg, st.astype(BF16)) * spread[lc:2 * lc, cols]
        for k in range(SSM_HPG):
            h = h0 + k
            seg = cs[:, h:h + 1] - cst[h:h + 1, :]
            decay = jnp.exp(jnp.where(lower, seg, -jnp.inf))
            m = (cb * decay).astype(BF16)
            y = y + _dot(m, jnp.where(head_of_lane == k, xdt_b, jnp.zeros_like(xdt_b)))
        x_end = (xg * spread[2 * lc:3 * lc, cols]).astype(BF16)
        bgt = bg.astype(F32).T.astype(BF16)
        state[g] = st * spread[3 * lc:3 * lc + 1, cols] + _dot(bgt, x_end)
        y = y + dskip_ref[:, g * gw:(g + 1) * gw] * xg
        y = y * jax.nn.silu(z_ref[:, g * gw:(g + 1) * gw].astype(F32))
        y = y * lax.rsqrt(jnp.mean(y * y, axis=-1, keepdims=True) + RMS_EPS)
        y_ref[:, g * gw:(g + 1) * gw] = (y * normw_ref[:, g * gw:(g + 1) * gw]).astype(BF16)


def _ssd(x, bm, cm, z, dt, dtt, n_tok, seq, alog, alogt, dskip, normw):
    lc = SSD_CHUNK
    nc = seq // lc
    tril = jnp.asarray(np.tril(np.ones((lc, lc), np.float32)))
    triu = jnp.asarray(np.triu(np.ones((lc, lc), np.float32)))
    head_of_col = np.arange(D_INNER) // SSM_HEAD_DIM
    spread = jnp.asarray(np.arange(2 * SSM_HEADS)[:, None] % SSM_HEADS == head_of_col[None, :], BF16)
    row = lambda b, c: (b * nc + c, 0)
    col = lambda b, c: (0, b * nc + c)
    fixed = lambda b, c: (0, 0)
    full = lambda a: pl.BlockSpec(a.shape, fixed)
    return pl.pallas_call(
        functools.partial(_ssd_kernel, lc=lc),
        grid=(n_tok // seq, nc),
        in_specs=[pl.BlockSpec((lc, D_INNER), row), pl.BlockSpec((lc, SSM_GN), row),
                  pl.BlockSpec((lc, SSM_GN), row), pl.BlockSpec((lc, D_INNER), row),
                  pl.BlockSpec((lc, SSM_HEADS), row), pl.BlockSpec((SSM_HEADS, lc), col),
                  full(alog), full(alogt), full(dskip), full(normw), full(tril), full(triu),
                  full(spread)],
        out_specs=pl.BlockSpec((lc, D_INNER), row),
        out_shape=jax.ShapeDtypeStruct((n_tok, D_INNER), BF16),
        scratch_shapes=[pltpu.VMEM((SSM_GROUPS, SSM_STATE, SSM_HPG * SSM_HEAD_DIM), F32)],
        compiler_params=_params("arbitrary", "arbitrary"),
        name="ssd_scan",
    )(x, bm, cm, z, dt, dtt, alog, alogt, dskip, normw, tril, triu, spread)


def _router_weights(wg, bg, we, be):
    w = jnp.concatenate([wg, jnp.moveaxis(we, 0, 1).reshape(D_MODEL, -1)], axis=1)
    b = jnp.concatenate([bg, be.reshape(-1)])
    pad = LANES - w.shape[1]
    w = jnp.pad(w, ((0, 0), (0, pad)))
    b = jnp.pad(b, (0, pad)).reshape(1, LANES)
    w_hi = w.astype(BF16)
    w_lo = (w - w_hi.astype(F32)).astype(BF16)
    return w_hi, w_lo, b


def kernel(x, even_w_in, even_conv_w, even_w_out, odd_in_proj, odd_conv_w, odd_conv_b, odd_dt_bias,
           odd_a_log, odd_d_skip, odd_norm_w, odd_out_proj, norm_mix, norm_ffn, router_g_w, router_g_b,
           router_e_w, router_e_b, expert_w_gate, expert_w_up, expert_w_down, final_norm):
    bsz, seq, d = x.shape
    n_tok = bsz * seq
    depth = norm_mix.shape[0]
    h = x.reshape(n_tok, d)
    fw = final_norm.reshape(1, d)
    for layer in range(depth):
        i = layer // 2
        nw = norm_mix[layer].reshape(1, d)
        if layer % 2 == 0:
            conv, q, k, v = _even_in_proj(h, n_tok, seq, nw, even_w_in[i].astype(BF16), even_conv_w[i])
            sb = _stick_breaking(q, k, v, n_tok, seq)
            w_out = even_w_out[i].astype(BF16)
            acts, ws = [conv, sb], [w_out[:CONV_WIDTH], w_out[CONV_WIDTH:]]
        else:
            w_in = odd_in_proj[i]
            wz = w_in[:, :D_INNER].astype(BF16)
            wx = w_in[:, D_INNER:D_INNER + SSM_CONV_DIM].astype(BF16)
            wdt = w_in[:, D_INNER + SSM_CONV_DIM:].astype(BF16)
            z, xs, bmat, cmat, dt, dtt = _odd_in_proj(
                h, n_tok, seq, nw, wz, wx, wdt, wdt.T, odd_conv_w[i], odd_conv_b[i].reshape(1, -1),
                odd_dt_bias[i].reshape(1, -1), odd_dt_bias[i].reshape(-1, 1))
            y = _ssd(xs, bmat, cmat, z, dt, dtt, n_tok, seq, odd_a_log[i].reshape(1, -1),
                     odd_a_log[i].reshape(-1, 1), jnp.repeat(odd_d_skip[i], SSM_HEAD_DIM).reshape(1, -1),
                     odd_norm_w[i].reshape(1, -1))
            acts, ws = [y], [odd_out_proj[i].astype(BF16)]
        wr_hi, wr_lo, br = _router_weights(router_g_w[layer], router_g_b[layer], router_e_w[layer],
                                           router_e_b[layer])
        h, route_i, route_f, counts = _mix_out_router(h, acts, ws, n_tok, norm_ffn[layer].reshape(1, d),
                                                      wr_hi, wr_lo, br)
        h = _moe_layer(h, n_tok, route_i, route_f, counts, norm_ffn[layer].reshape(1, d),
                       layer, expert_w_gate, expert_w_up, expert_w_down, fw, final_norm=(layer == depth - 1))
    return h[:n_tok].reshape(bsz, seq, d)
```

```python
import functools

import numpy as np
import jax
import jax.numpy as jnp
from jax import lax
from jax.experimental import pallas as pl
from jax.experimental.pallas import tpu as pltpu

F32 = jnp.float32
BF16 = jnp.bfloat16

D_MODEL = 1024
RMS_EPS = 1e-6
CONV_WIDTH = 512
SHORT_CONV_K = 3
SB_HEAD_DIM = 64
SB_WIDTH = 512
D_INNER = 2048
SSM_HEAD_DIM = 64
SSM_HEADS = 32
SSM_GROUPS = 8
SSM_HPG = 4
SSM_STATE = 128
SSM_CONV_K = 4
SSM_GN = SSM_GROUPS * SSM_STATE
SSM_CONV_DIM = D_INNER + 2 * SSM_GN
N_EXPERT_GROUPS = 4
EXPERTS_PER_GROUP = 8
EXPERT_FF = 512
N_PAIRS = EXPERTS_PER_GROUP * (EXPERTS_PER_GROUP - 1) // 2
N_BUCKETS = N_EXPERT_GROUPS * N_PAIRS

LANES = 128
SUBLANES = 8
N_LANE_TILES = D_MODEL // LANES
VMEM_LIMIT = 56 * 1024 * 1024

ROW_TILE = 512
SB_BLOCK = 256
SB_GROUP = 4
SSD_CHUNK = 128
MOE_BLOCK = 320


def _params(*sem):
    return pltpu.CompilerParams(dimension_semantics=sem, vmem_limit_bytes=VMEM_LIMIT)


def _rms(x, w):
    return x * lax.rsqrt(jnp.mean(x * x, axis=-1, keepdims=True) + RMS_EPS) * w


def _dot(a, b):
    return jnp.dot(a, b, preferred_element_type=F32)


def _load_rows(ref):
    if ref.shape[1] == D_MODEL:
        return ref[...]
    n = ref.shape[0] // N_LANE_TILES
    return jnp.concatenate([ref[pl.ds(s, n, stride=N_LANE_TILES), :] for s in range(N_LANE_TILES)], axis=1)


def _store_rows(ref, val):
    if ref.shape[1] == D_MODEL:
        ref[...] = val
    else:
        n = ref.shape[0] // N_LANE_TILES
        for s in range(N_LANE_TILES):
            ref[pl.ds(s, n, stride=N_LANE_TILES), :] = val[:, s * LANES:(s + 1) * LANES]


def _rows_spec(h, tm):
    if h.shape[1] == D_MODEL:
        return pl.BlockSpec((tm, D_MODEL), lambda i: (i, 0))
    return pl.BlockSpec((tm * N_LANE_TILES, LANES), lambda i: (i, 0))


def _dot_nt(a, b):
    return lax.dot_general(a, b, (((1,), (1,)), ((), ())), preferred_element_type=F32)


def _even_in_kernel(h_ref, nw_ref, w_ref, cw_ref, conv_ref, q_ref, k_ref, v_ref, ubuf,
                    *, tm, tiles_per_seq):
    i = pl.program_id(0)
    xb = _rms(_load_rows(h_ref), nw_ref[...]).astype(BF16)
    c = CONV_WIDTH

    def proj(c0):
        return _dot(xb, w_ref[:, c0:c0 + c])

    gate_b = proj(0)
    u = proj(c) * proj(2 * c)

    @pl.when(i % tiles_per_seq == 0)
    def _():
        ubuf[0:SUBLANES, :] = jnp.zeros((SUBLANES, c), F32)

    ubuf[SUBLANES:SUBLANES + tm, :] = u
    cw = cw_ref[...]
    conv = cw[2:3, :] * u + cw[1:2, :] * ubuf[SUBLANES - 1:SUBLANES - 1 + tm, :] \
        + cw[0:1, :] * ubuf[SUBLANES - 2:SUBLANES - 2 + tm, :]
    conv_ref[...] = (gate_b * conv).astype(BF16)
    ubuf[0:SUBLANES, :] = ubuf[tm:tm + SUBLANES, :]
    q_ref[...] = (proj(3 * c) * (SB_HEAD_DIM ** -0.5)).astype(BF16)
    k_ref[...] = proj(4 * c).astype(BF16)
    v_ref[...] = proj(5 * c).astype(BF16)


def _even_in_proj(h, n_tok, seq, nw, w_in, conv_w):
    tm = ROW_TILE
    c = CONV_WIDTH
    row = lambda i: (i, 0)
    fixed = lambda i: (0, 0)
    out = jax.ShapeDtypeStruct((n_tok, c), BF16)
    return pl.pallas_call(
        functools.partial(_even_in_kernel, tm=tm, tiles_per_seq=seq // tm),
        grid=(n_tok // tm,),
        in_specs=[_rows_spec(h, tm), pl.BlockSpec((1, D_MODEL), fixed),
                  pl.BlockSpec((D_MODEL, 6 * c), fixed), pl.BlockSpec((SHORT_CONV_K, c), fixed)],
        out_specs=[pl.BlockSpec((tm, c), row)] * 4,
        out_shape=[out] * 4,
        scratch_shapes=[pltpu.VMEM((tm + SUBLANES, c), F32)],
        compiler_params=_params("arbitrary"),
        name="even_in_proj",
    )(h, nw, w_in, conv_w)


def _sb_kernel(q_ref, k_ref, v_ref, tri_ref, o_ref, *, seq, bq, group):
    nq = seq // bq
    lane = lax.broadcasted_iota(jnp.int32, (bq, LANES), 1)
    rr = lax.broadcasted_iota(jnp.int32, (bq, bq), 0)
    cc = lax.broadcasted_iota(jnp.int32, (bq, bq), 1)
    causal = cc < rr
    every = cc >= 0
    tri = tri_ref[...]

    def block(qs, kb, vb, run, acc, mask):
        z = _dot_nt(qs, kb)
        sp = jnp.maximum(z, 0.0) + jnp.log(1.0 + jnp.exp(-jnp.abs(z)))
        spm = sp if mask is None else jnp.where(mask, sp, 0.0)
        suffix = _dot(spm.astype(BF16), tri) + run
        p = jnp.exp((z - sp) + suffix)
        if mask is not None:
            p = jnp.where(mask, p, 0.0)
        acc = acc + _dot(p.astype(BF16), vb)
        run = run - jnp.sum(spm, axis=1, keepdims=True)
        return run, acc

    def kv(kj):
        rows = pl.ds(pl.multiple_of(kj * bq, bq), bq)
        return k_ref[rows, :], v_ref[rows, :]

    def heads(qi):
        q = q_ref[qi * bq:(qi + 1) * bq, :]
        return [jnp.where((lane < SB_HEAD_DIM) == (head == 0), q, jnp.zeros_like(q)) for head in range(2)]

    def store(qi, acc):
        o_ref[qi * bq:(qi + 1) * bq, :] = jnp.where(lane < SB_HEAD_DIM, acc[:bq], acc[bq:]).astype(BF16)

    for q0 in range(0, nq, group):
        qhs = [heads(q0 + j) for j in range(group)]
        run = acc = None
        for kk in range(group - 1, -1, -1):
            kb, vb = kv(q0 + kk)
            qs = jnp.concatenate([q for j in range(kk, group) for q in qhs[j]], axis=0)
            mask = jnp.concatenate([causal, causal] + [every] * (2 * (group - 1 - kk)), axis=0)
            run0 = jnp.zeros((2 * bq, 1), F32)
            acc0 = jnp.zeros((2 * bq, LANES), F32)
            run = run0 if run is None else jnp.concatenate([run0, run], axis=0)
            acc = acc0 if acc is None else jnp.concatenate([acc0, acc], axis=0)
            run, acc = block(qs, kb, vb, run, acc, mask)
        if q0 > 0:
            def body(it, cr, qs=qs, q0=q0):
                kb, vb = kv(q0 - 1 - it)
                return block(qs, kb, vb, cr[0], cr[1], None)
            run, acc = lax.fori_loop(0, q0, body, (run, acc))
        for j in range(group):
            store(q0 + j, acc[2 * j * bq:2 * (j + 1) * bq])


def _stick_breaking(q, k, v, n_tok, seq):
    bq = SB_BLOCK
    tri = jnp.asarray(-np.tril(np.ones((bq, bq), np.float32), -1), BF16)
    blk = pl.BlockSpec((seq, LANES), lambda b, hp: (b, hp))
    return pl.pallas_call(
        functools.partial(_sb_kernel, seq=seq, bq=bq, group=SB_GROUP),
        grid=(n_tok // seq, SB_WIDTH // LANES),
        in_specs=[blk, blk, blk, pl.BlockSpec((bq, bq), lambda b, hp: (0, 0))],
        out_specs=blk,
        out_shape=jax.ShapeDtypeStruct((n_tok, SB_WIDTH), BF16),
        compiler_params=_params("arbitrary", "arbitrary"),
        name="stick_breaking",
    )(q, k, v, tri)


def _mix_out_kernel(*refs, n_act, tm):
    h_ref = refs[0]
    act_refs = refs[1:1 + n_act]
    w_refs = refs[1 + n_act:1 + 2 * n_act]
    nw_ref, wrh_ref, wrl_ref, br_ref, triu_ref = refs[1 + 2 * n_act:6 + 2 * n_act]
    hout_ref, ri_ref, rf_ref, cnt_ref, carry = refs[6 + 2 * n_act:]
    i = pl.program_id(0)

    @pl.when(i == 0)
    def _():
        carry[...] = jnp.zeros(carry.shape, F32)

    acc = _load_rows(h_ref)
    for a_ref, w_ref in zip(act_refs, w_refs):
        acc = acc + _dot(a_ref[...], w_ref[...])
    _store_rows(hout_ref, acc)

    xn = _rms(acc, nw_ref[...])
    xh = xn.astype(BF16)
    xl = (xn - xh.astype(F32)).astype(BF16)
    wrh = wrh_ref[...]
    logits = _dot(xh, wrh) + _dot(xl, wrh) + _dot(xh, wrl_ref[...]) + br_ref[...]
    lt = logits.T
    r = lax.broadcasted_iota(jnp.int32, (LANES, tm), 0).astype(F32)
    big = float(LANES)
    neg = -jnp.inf

    def first_argmax(vals):
        m = jnp.max(vals, axis=0, keepdims=True)
        return m, jnp.min(jnp.where(vals == m, r, big), axis=0, keepdims=True)

    gl = jnp.where(r < N_EXPERT_GROUPS, lt, neg)
    gmax, gidx = first_argmax(gl)
    g_top_p = 1.0 / jnp.sum(jnp.exp(gl - gmax), axis=0, keepdims=True)
    lo = N_EXPERT_GROUPS + EXPERTS_PER_GROUP * gidx
    el = jnp.where((r >= lo) & (r < lo + EXPERTS_PER_GROUP), lt, neg)
    m1, i1 = first_argmax(el)
    m2, i2 = first_argmax(jnp.where(r == i1, neg, el))
    t = jnp.exp(m2 - m1)
    g1 = g_top_p / (1.0 + t)
    g2 = g_top_p * t / (1.0 + t)
    e1 = i1 - lo
    e2 = i2 - lo
    ea = jnp.minimum(e1, e2)
    eb = jnp.maximum(e1, e2)
    pair = ea * (2 * EXPERTS_PER_GROUP - 1 - ea) * 0.5 + (eb - ea - 1.0)
    bucket = gidx * N_PAIRS + pair
    first_is_a = e1 < e2
    gate_a = jnp.where(first_is_a, g1, g2)
    gate_b = jnp.where(first_is_a, g2, g1)

    onehot = r == bucket
    cum = _dot(jnp.where(onehot, 1.0, 0.0).astype(BF16), triu_ref[...])
    prev = carry[...]
    rank = jnp.sum(jnp.where(onehot, cum + prev[:, 0:1], 0.0), axis=0, keepdims=True)
    new = prev + jnp.sum(jnp.where(onehot, 1.0, 0.0), axis=1, keepdims=True)
    carry[...] = new
    cnt_ref[...] = new

    row8 = lax.broadcasted_iota(jnp.int32, (SUBLANES, tm), 0)
    ri_ref[...] = jnp.where(row8 == 0, bucket, jnp.where(row8 == 1, rank, 0.0)).astype(jnp.int32)
    rf_ref[...] = jnp.where(row8 == 0, gate_a, jnp.where(row8 == 1, gate_b, 0.0))


def _mix_out_router(h, acts, ws, n_tok, nw, wr_hi, wr_lo, br):
    tm = ROW_TILE
    row = lambda i: (i, 0)
    col = lambda i: (0, i)
    fixed = lambda i: (0, 0)
    triu = jnp.asarray(np.triu(np.ones((tm, tm), np.float32), 1), BF16)
    in_specs = [_rows_spec(h, tm)]
    in_specs += [pl.BlockSpec((tm, a.shape[1]), row) for a in acts]
    in_specs += [pl.BlockSpec(w.shape, fixed) for w in ws]
    in_specs += [pl.BlockSpec((1, D_MODEL), fixed), pl.BlockSpec((D_MODEL, LANES), fixed),
                 pl.BlockSpec((D_MODEL, LANES), fixed), pl.BlockSpec((1, LANES), fixed),
                 pl.BlockSpec((tm, tm), fixed)]
    return pl.pallas_call(
        functools.partial(_mix_out_kernel, n_act=len(acts), tm=tm),
        grid=(n_tok // tm,),
        in_specs=in_specs,
        out_specs=[pl.BlockSpec((tm * N_LANE_TILES, LANES), row), pl.BlockSpec((SUBLANES, tm), col),
                   pl.BlockSpec((SUBLANES, tm), col), pl.BlockSpec((LANES, LANES), fixed)],
        out_shape=[jax.ShapeDtypeStruct((n_tok * N_LANE_TILES, LANES), F32),
                   jax.ShapeDtypeStruct((SUBLANES, n_tok), jnp.int32),
                   jax.ShapeDtypeStruct((SUBLANES, n_tok), F32),
                   jax.ShapeDtypeStruct((LANES, LANES), F32)],
        scratch_shapes=[pltpu.VMEM((LANES, LANES), F32)],
        compiler_params=_params("arbitrary"),
        name="mix_out_router",
    )(h, *acts, *ws, nw, wr_hi, wr_lo, br, triu)


def _ffn_kernel(nused_ref, ea_ref, eb_ref, gcur_ref, gnext_ref, sprev_ref, g_ref, nw_ref,
                wga_ref, wua_ref, wda_ref, wgb_ref, wub_ref, wdb_ref, fw_ref, h_hbm,
                out_hbm, xbuf0, xbuf1, obuf0, obuf1, gsem, ssem, *, bm, n_tok, rpo, final_norm):
    del ea_ref, eb_ref
    i = pl.program_id(0)
    slot = i % 2
    nused = nused_ref[0]
    xbufs = (xbuf0, xbuf1)
    obufs = (obuf0, obuf1)

    rpi = N_LANE_TILES

    def tok_rows(t, n):
        return pl.ds(t * n if isinstance(t, int) else pl.multiple_of(t * n, n), n)

    def row_in(gref, s, r):
        return pltpu.make_async_copy(h_hbm.at[tok_rows(gref[0, 0, r], rpi)], xbufs[s].at[tok_rows(r, rpi)],
                                     gsem.at[s])

    def row_out(s, r):
        return pltpu.make_async_copy(obufs[s].at[tok_rows(r, rpo)], out_hbm.at[tok_rows(sprev_ref[0, 0, r], rpo)],
                                     ssem.at[s])

    def wait_gather(s):
        pltpu.make_async_copy(h_hbm.at[pl.ds(0, bm * rpi)], xbufs[s], gsem.at[s]).wait()

    def wait_scatter(s):
        pltpu.make_async_copy(obufs[s], out_hbm.at[pl.ds(0, bm * rpo)], ssem.at[s]).wait()

    @pl.when(i == 0)
    def _():
        def body(r, c):
            row_in(gcur_ref, 0, r).start()
            return c
        lax.fori_loop(0, bm, body, 0, unroll=8)
        for s in range(2):
            obufs[s][...] = jnp.zeros(obufs[s].shape, F32)
            cp = pltpu.make_async_copy(obufs[s], out_hbm.at[pl.ds((n_tok + s * bm) * rpo, bm * rpo)], ssem.at[s])
            cp.start()
            if s == 1:
                cp.wait()

    def run_block(cur, nxt):
        for r in range(bm):
            row_in(gnext_ref, nxt, r).start()
        wait_gather(cur)
        x = _load_rows(xbufs[cur])
        xb = _rms(x, nw_ref[...]).astype(BF16)
        g = g_ref[...]

        def hidden(wg_ref, wu_ref, gate):
            return (jax.nn.silu(_dot(xb, wg_ref[0, 0].astype(BF16))) * _dot(xb, wu_ref[0, 0].astype(BF16))
                    * gate).astype(BF16)

        hid_a = hidden(wga_ref, wua_ref, g[:, 0:1])
        hid_b = hidden(wgb_ref, wub_ref, g[:, 1:2])
        for r in range(bm):
            row_out(nxt, r).start()
        wait_scatter(cur)
        y = x + _dot(hid_a, wda_ref[0, 0].astype(BF16)) + _dot(hid_b, wdb_ref[0, 0].astype(BF16))
        if final_norm:
            y = _rms(y, fw_ref[...])
        _store_rows(obufs[cur], y)

    for s in range(2):
        @pl.when((i < nused) & (slot == s))
        def _(s=s):
            run_block(s, 1 - s)

    @pl.when(i == nused)
    def _():
        for s in range(2):
            @pl.when(slot == s)
            def _(s=s):
                wait_gather(s)
                wait_scatter(s)

                def body(r, c):
                    row_out(1 - s, r).start()
                    return c
                lax.fori_loop(0, bm, body, 0, unroll=8)
                wait_scatter(1 - s)


def _moe_ffn(h, n_tok, nused, blk_ea, blk_eb, gather_rows, scatter_rows, gate_rows, nw, layer, wg, wu, wd, fw,
             final_norm):
    bm = MOE_BLOCK
    n_blocks = gather_rows.shape[0]
    rpo = 1 if final_norm else N_LANE_TILES
    out_cols = D_MODEL // rpo
    cur = lambda i, nu, ea, eb: (i, 0, 0)
    nxt = lambda i, nu, ea, eb: (jnp.minimum(i + 1, n_blocks - 1), 0, 0)
    idx_spec = lambda imap: pl.BlockSpec((1, 1, bm), imap, memory_space=pltpu.SMEM)
    fixed = lambda i, nu, ea, eb: (0, 0)
    wa = lambda i, nu, ea, eb: (layer, ea[i], 0, 0)
    wb = lambda i, nu, ea, eb: (layer, eb[i], 0, 0)
    up = pl.BlockSpec((1, 1, D_MODEL, EXPERT_FF), wa)
    upb = pl.BlockSpec((1, 1, D_MODEL, EXPERT_FF), wb)
    grid_spec = pltpu.PrefetchScalarGridSpec(
        num_scalar_prefetch=3,
        grid=(n_blocks,),
        in_specs=[idx_spec(cur), idx_spec(nxt), idx_spec(cur),
                  pl.BlockSpec((bm, 2), lambda i, nu, ea, eb: (i, 0)),
                  pl.BlockSpec((1, D_MODEL), fixed),
                  up, up, pl.BlockSpec((1, 1, EXPERT_FF, D_MODEL), wa),
                  upb, upb, pl.BlockSpec((1, 1, EXPERT_FF, D_MODEL), wb),
                  pl.BlockSpec((1, D_MODEL), fixed),
                  pl.BlockSpec(memory_space=pl.ANY)],
        out_specs=pl.BlockSpec(memory_space=pl.ANY),
        scratch_shapes=[pltpu.VMEM((bm * N_LANE_TILES, LANES), F32)] * 2 + [pltpu.VMEM((bm * rpo, out_cols), F32)] * 2
        + [pltpu.SemaphoreType.DMA((2,)), pltpu.SemaphoreType.DMA((2,))],
    )
    return pl.pallas_call(
        functools.partial(_ffn_kernel, bm=bm, n_tok=n_tok, rpo=rpo, final_norm=final_norm),
        grid_spec=grid_spec,
        out_shape=jax.ShapeDtypeStruct(((n_tok + 2 * bm) * rpo, out_cols), F32),
        compiler_params=_params("arbitrary"),
        name="moe_ffn",
    )(nused, blk_ea, blk_eb, gather_rows, gather_rows, scatter_rows, gate_rows, nw, wg, wu, wd, wg, wu, wd,
      fw, h)


_PAIR_A = np.array([a for a in range(EXPERTS_PER_GROUP) for b in range(a + 1, EXPERTS_PER_GROUP)], np.int32)
_PAIR_B = np.array([b for a in range(EXPERTS_PER_GROUP) for b in range(a + 1, EXPERTS_PER_GROUP)], np.int32)


def _moe_layer(h, n_tok, route_i, route_f, counts, nw, layer, wg, wu, wd, fw, final_norm):
    bm = MOE_BLOCK
    n_blocks = n_tok // bm + N_BUCKETS + 1
    bucket = route_i[0]
    rank = route_i[1]
    cnt = counts[:N_BUCKETS, 0].astype(jnp.int32)
    nblk = (cnt + bm - 1) // bm
    blk_end = jnp.cumsum(nblk)
    blk_start = blk_end - nblk
    in_bucket = bucket[:, None] == jnp.arange(N_BUCKETS, dtype=jnp.int32)[None, :]
    dest = jnp.sum(jnp.where(in_bucket, blk_start[None, :], 0), axis=1) * bm + rank
    upd = jnp.stack([jnp.arange(n_tok, dtype=jnp.int32), lax.bitcast_convert_type(route_f[0], jnp.int32),
                     lax.bitcast_convert_type(route_f[1], jnp.int32)], axis=1)
    init = jnp.broadcast_to(jnp.array([n_tok, 0, 0], jnp.int32), (n_blocks * bm, 3))
    packed = init.at[dest].set(upd, unique_indices=True)
    tok_rows = packed[:, 0]
    gate_rows = lax.bitcast_convert_type(packed[:, 1:3], F32)
    blk_bucket = jnp.minimum(
        jnp.searchsorted(blk_end, jnp.arange(n_blocks, dtype=jnp.int32), side='right'),
        N_BUCKETS - 1).astype(jnp.int32)
    nused = blk_end[-1:].astype(jnp.int32)
    last = blk_bucket[jnp.maximum(nused[0] - 1, 0)]
    blk_bucket = jnp.where(jnp.arange(n_blocks) < nused[0], blk_bucket, last)
    grp = blk_bucket // N_PAIRS
    pair = blk_bucket % N_PAIRS
    blk_ea = grp * EXPERTS_PER_GROUP + jnp.asarray(_PAIR_A)[pair]
    blk_eb = grp * EXPERTS_PER_GROUP + jnp.asarray(_PAIR_B)[pair]
    row = jnp.arange(n_blocks * bm, dtype=jnp.int32)
    gather_rows = jnp.minimum(tok_rows, n_tok - 1)
    scatter_rows = jnp.where(tok_rows >= n_tok, n_tok + (row // bm) % 2 * bm + row % bm, tok_rows)
    scatter_rows = jnp.concatenate([n_tok + bm + row[:bm], scatter_rows])
    return _moe_ffn(h, n_tok, nused, blk_ea, blk_eb, gather_rows.reshape(n_blocks, 1, bm),
                    scatter_rows.reshape(n_blocks + 1, 1, bm), gate_rows, nw, layer, wg, wu, wd, fw,
                    final_norm)


def _odd_in_kernel(h_ref, nw_ref, wz_ref, wx_ref, wdt_ref, wdtt_ref, cw_ref, cb_ref, dtb_ref, dtbt_ref,
                   z_ref, x_ref, b_ref, c_ref, dt_ref, dtt_ref, ubuf, *, tm, tiles_per_seq):
    i = pl.program_id(0)
    xb = _rms(_load_rows(h_ref), nw_ref[...]).astype(BF16)
    cw_all = cw_ref[...]
    cb_all = cb_ref[...]
    wchunk = 512

    @pl.when(i % tiles_per_seq == 0)
    def _():
        ubuf[0:SUBLANES, :] = jnp.zeros((SUBLANES, SSM_CONV_DIM), F32)

    for c0 in range(0, D_INNER, wchunk):
        z_ref[:, c0:c0 + wchunk] = _dot(xb, wz_ref[:, c0:c0 + wchunk]).astype(BF16)

    for c0 in range(0, SSM_CONV_DIM, wchunk):
        cols = slice(c0, c0 + wchunk)
        u = _dot(xb, wx_ref[:, cols])
        ubuf[SUBLANES:SUBLANES + tm, cols] = u
        cw = cw_all[:, cols]
        conv = cw[3:4, :] * u + cb_all[:, cols]
        for j in range(1, SSM_CONV_K):
            conv = conv + cw[3 - j:4 - j, :] * ubuf[SUBLANES - j:SUBLANES - j + tm, cols]
        act = jax.nn.silu(conv).astype(BF16)
        if c0 < D_INNER:
            x_ref[:, c0:c0 + wchunk] = act
        elif c0 < D_INNER + SSM_GN:
            b_ref[:, c0 - D_INNER:c0 - D_INNER + wchunk] = act
        else:
            c_ref[:, c0 - D_INNER - SSM_GN:c0 - D_INNER - SSM_GN + wchunk] = act
    ubuf[0:SUBLANES, :] = ubuf[tm:tm + SUBLANES, :]

    dt_ref[...] = jax.nn.softplus(_dot(xb, wdt_ref[...]) + dtb_ref[...])
    dtt_ref[...] = jax.nn.softplus(_dot_nt(wdtt_ref[...], xb) + dtbt_ref[...])


def _odd_in_proj(h, n_tok, seq, nw, wz, wx, wdt, wdtt, conv_w, conv_b, dtb, dtbt):
    tm = ROW_TILE
    row = lambda i: (i, 0)
    col = lambda i: (0, i)
    fixed = lambda i: (0, 0)
    full = lambda a: pl.BlockSpec(a.shape, fixed)
    return pl.pallas_call(
        functools.partial(_odd_in_kernel, tm=tm, tiles_per_seq=seq // tm),
        grid=(n_tok // tm,),
        in_specs=[_rows_spec(h, tm), full(nw), full(wz), full(wx), full(wdt), full(wdtt),
                  full(conv_w), full(conv_b), full(dtb), full(dtbt)],
        out_specs=[pl.BlockSpec((tm, D_INNER), row), pl.BlockSpec((tm, D_INNER), row),
                   pl.BlockSpec((tm, SSM_GN), row), pl.BlockSpec((tm, SSM_GN), row),
                   pl.BlockSpec((tm, SSM_HEADS), row), pl.BlockSpec((SSM_HEADS, tm), col)],
        out_shape=[jax.ShapeDtypeStruct((n_tok, D_INNER), BF16), jax.ShapeDtypeStruct((n_tok, D_INNER), BF16),
                   jax.ShapeDtypeStruct((n_tok, SSM_GN), BF16), jax.ShapeDtypeStruct((n_tok, SSM_GN), BF16),
                   jax.ShapeDtypeStruct((n_tok, SSM_HEADS), F32), jax.ShapeDtypeStruct((SSM_HEADS, n_tok), F32)],
        scratch_shapes=[pltpu.VMEM((tm + SUBLANES, SSM_CONV_DIM), F32)],
        compiler_params=_params("arbitrary"),
        name="odd_in_proj",
    )(h, nw, wz, wx, wdt, wdtt, conv_w, conv_b, dtb, dtbt)


def _ssd_kernel(x_ref, b_ref, c_ref, z_ref, dt_ref, dtt_ref, alog_ref, alogt_ref, dskip_ref, normw_ref,
                tril_ref, triu_ref, spread_ref, y_ref, state, *, lc):
    ci = pl.program_id(1)

    @pl.when(ci == 0)
    def _():
        state[...] = jnp.zeros(state.shape, F32)

    gw = SSM_HPG * SSM_HEAD_DIM
    hi = lax.Precision.HIGHEST
    dt = dt_ref[...]
    dtt = dtt_ref[...]
    a = -jnp.exp(alog_ref[...])
    at = -jnp.exp(alogt_ref[...])
    cs = jnp.dot(tril_ref[...], dt * a, precision=hi, preferred_element_type=F32)
    cst = jnp.dot(dtt * at, triu_ref[...], precision=hi, preferred_element_type=F32)
    cs_last = cs[lc - 1:lc, :]
    grow = jnp.exp(cs)
    dt_end = dt * jnp.exp(cs_last - cs)
    e_last = jnp.exp(cs_last)
    rr = lax.broadcasted_iota(jnp.int32, (lc, lc), 0)
    cc = lax.broadcasted_iota(jnp.int32, (lc, lc), 1)
    lower = cc <= rr
    lane = lax.broadcasted_iota(jnp.int32, (lc, gw), 1)
    head_of_lane = lane // SSM_HEAD_DIM
    fac = jnp.concatenate([dt, grow, dt_end, jnp.broadcast_to(e_last, (SUBLANES, SSM_HEADS))], axis=0)
    fac_hi = fac.astype(BF16).astype(F32)
    spread = _dot(jnp.concatenate([fac_hi, fac - fac_hi], axis=1).astype(BF16), spread_ref[...])

    for g in range(SSM_GROUPS):
        h0 = g * SSM_HPG
        bg = b_ref[:, g * SSM_STATE:(g + 1) * SSM_STATE]
        cg = c_ref[:, g * SSM_STATE:(g + 1) * SSM_STATE]
        xg = x_ref[:, g * gw:(g + 1) * gw].astype(F32)
        cb = _dot_nt(cg, bg)
        cols = slice(g * gw, (g + 1) * gw)
        xdt = xg * spread[0:lc, cols]
        xdt_b = xdt.astype(BF16)
        st = state[g]
        y = _dot(cg, st.astype(BF16)) * spread[lc:2 * lc, cols]
        for k in range(SSM_HPG):
            h = h0 + k
            seg = cs[:, h:h + 1] - cst[h:h + 1, :]
            decay = jnp.exp(jnp.where(lower, seg, -jnp.inf))
            m = (cb * decay).astype(BF16)
            y = y + _dot(m, jnp.where(head_of_lane == k, xdt_b, jnp.zeros_like(xdt_b)))
        x_end = (xg * spread[2 * lc:3 * lc, cols]).astype(BF16)
        bgt = bg.astype(F32).T.astype(BF16)
        state[g] = st * spread[3 * lc:3 * lc + 1, cols] + _dot(bgt, x_end)
        y = y + dskip_ref[:, g * gw:(g + 1) * gw] * xg
        y = y * jax.nn.silu(z_ref[:, g * gw:(g + 1) * gw].astype(F32))
        y = y * lax.rsqrt(jnp.mean(y * y, axis=-1, keepdims=True) + RMS_EPS)
        y_ref[:, g * gw:(g + 1) * gw] = (y * normw_ref[:, g * gw:(g + 1) * gw]).astype(BF16)


def _ssd(x, bm, cm, z, dt, dtt, n_tok, seq, alog, alogt, dskip, normw):
    lc = SSD_CHUNK
    nc = seq // lc
    tril = jnp.asarray(np.tril(np.ones((lc, lc), np.float32)))
    triu = jnp.asarray(np.triu(np.ones((lc, lc), np.float32)))
    head_of_col = np.arange(D_INNER) // SSM_HEAD_DIM
    spread = jnp.asarray(np.arange(2 * SSM_HEADS)[:, None] % SSM_HEADS == head_of_col[None, :], BF16)
    row = lambda b, c: (b * nc + c, 0)
    col = lambda b, c: (0, b * nc + c)
    fixed = lambda b, c: (0, 0)
    full = lambda a: pl.BlockSpec(a.shape, fixed)
    return pl.pallas_call(
        functools.partial(_ssd_kernel, lc=lc),
        grid=(n_tok // seq, nc),
        in_specs=[pl.BlockSpec((lc, D_INNER), row), pl.BlockSpec((lc, SSM_GN), row),
                  pl.BlockSpec((lc, SSM_GN), row), pl.BlockSpec((lc, D_INNER), row),
                  pl.BlockSpec((lc, SSM_HEADS), row), pl.BlockSpec((SSM_HEADS, lc), col),
                  full(alog), full(alogt), full(dskip), full(normw), full(tril), full(triu),
                  full(spread)],
        out_specs=pl.BlockSpec((lc, D_INNER), row),
        out_shape=jax.ShapeDtypeStruct((n_tok, D_INNER), BF16),
        scratch_shapes=[pltpu.VMEM((SSM_GROUPS, SSM_STATE, SSM_HPG * SSM_HEAD_DIM), F32)],
        compiler_params=_params("arbitrary", "arbitrary"),
        name="ssd_scan",
    )(x, bm, cm, z, dt, dtt, alog, alogt, dskip, normw, tril, triu, spread)


def _router_weights(wg, bg, we, be):
    w = jnp.concatenate([wg, jnp.moveaxis(we, 0, 1).reshape(D_MODEL, -1)], axis=1)
    b = jnp.concatenate([bg, be.reshape(-1)])
    pad = LANES - w.shape[1]
    w = jnp.pad(w, ((0, 0), (0, pad)))
    b = jnp.pad(b, (0, pad)).reshape(1, LANES)
    w_hi = w.astype(BF16)
    w_lo = (w - w_hi.astype(F32)).astype(BF16)
    return w_hi, w_lo, b


def kernel(x, even_w_in, even_conv_w, even_w_out, odd_in_proj, odd_conv_w, odd_conv_b, odd_dt_bias,
           odd_a_log, odd_d_skip, odd_norm_w, odd_out_proj, norm_mix, norm_ffn, router_g_w, router_g_b,
           router_e_w, router_e_b, expert_w_gate, expert_w_up, expert_w_down, final_norm):
    bsz, seq, d = x.shape
    n_tok = bsz * seq
    depth = norm_mix.shape[0]
    h = x.reshape(n_tok, d)
    fw = final_norm.reshape(1, d)
    for layer in range(depth):
        i = layer // 2
        nw = norm_mix[layer].reshape(1, d)
        if layer % 2 == 0:
            conv, q, k, v = _even_in_proj(h, n_tok, seq, nw, even_w_in[i].astype(BF16), even_conv_w[i])
            sb = _stick_breaking(q, k, v, n_tok, seq)
            w_out = even_w_out[i].astype(BF16)
            acts, ws = [conv, sb], [w_out[:CONV_WIDTH], w_out[CONV_WIDTH:]]
        else:
            w_in = odd_in_proj[i]
            wz = w_in[:, :D_INNER].astype(BF16)
            wx = w_in[:, D_INNER:D_INNER + SSM_CONV_DIM].astype(BF16)
            wdt = w_in[:, D_INNER + SSM_CONV_DIM:].astype(BF16)
            z, xs, bmat, cmat, dt, dtt = _odd_in_proj(
                h, n_tok, seq, nw, wz, wx, wdt, wdt.T, odd_conv_w[i], odd_conv_b[i].reshape(1, -1),
                odd_dt_bias[i].reshape(1, -1), odd_dt_bias[i].reshape(-1, 1))
            y = _ssd(xs, bmat, cmat, z, dt, dtt, n_tok, seq, odd_a_log[i].reshape(1, -1),
                     odd_a_log[i].reshape(-1, 1), jnp.repeat(odd_d_skip[i], SSM_HEAD_DIM).reshape(1, -1),
                     odd_norm_w[i].reshape(1, -1))
            acts, ws = [y], [odd_out_proj[i].astype(BF16)]
        wr_hi, wr_lo, br = _router_weights(router_g_w[layer], router_g_b[layer], router_e_w[layer],
                                           router_e_b[layer])
        h, route_i, route_f, counts = _mix_out_router(h, acts, ws, n_tok, norm_ffn[layer].reshape(1, d),
                                                      wr_hi, wr_lo, br)
        h = _moe_layer(h, n_tok, route_i, route_f, counts, norm_ffn[layer].reshape(1, d),
                       layer, expert_w_gate, expert_w_up, expert_w_down, fw, final_norm=(layer == depth - 1))
    return h[:n_tok].reshape(bsz, seq, d)
```

```python
import functools

import numpy as np
import jax
import jax.numpy as jnp
from jax import lax
from jax.experimental import pallas as pl
from jax.experimental.pallas import tpu as pltpu

F32 = jnp.float32
BF16 = jnp.bfloat16

D_MODEL = 1024
RMS_EPS = 1e-6
CONV_WIDTH = 512
SHORT_CONV_K = 3
SB_HEAD_DIM = 64
SB_WIDTH = 512
D_INNER = 2048
SSM_HEAD_DIM = 64
SSM_HEADS = 32
SSM_GROUPS = 8
SSM_HPG = 4
SSM_STATE = 128
SSM_CONV_K = 4
SSM_GN = SSM_GROUPS * SSM_STATE
SSM_CONV_DIM = D_INNER + 2 * SSM_GN
N_EXPERT_GROUPS = 4
EXPERTS_PER_GROUP = 8
EXPERT_FF = 512
N_PAIRS = EXPERTS_PER_GROUP * (EXPERTS_PER_GROUP - 1) // 2
N_BUCKETS = N_EXPERT_GROUPS * N_PAIRS

LANES = 128
SUBLANES = 8
N_LANE_TILES = D_MODEL // LANES
VMEM_LIMIT = 56 * 1024 * 1024

ROW_TILE = 512
SB_BLOCK = 256
SB_GROUP = 4
SSD_CHUNK = 128
MOE_BLOCK = 320


def _params(*sem):
    return pltpu.CompilerParams(dimension_semantics=sem, vmem_limit_bytes=VMEM_LIMIT)


def _rms(x, w):
    return x * lax.rsqrt(jnp.mean(x * x, axis=-1, keepdims=True) + RMS_EPS) * w


def _dot(a, b):
    return jnp.dot(a, b, preferred_element_type=F32)


def _load_rows(ref):
    if ref.shape[1] == D_MODEL:
        return ref[...]
    n = ref.shape[0] // N_LANE_TILES
    return jnp.concatenate([ref[pl.ds(s, n, stride=N_LANE_TILES), :] for s in range(N_LANE_TILES)], axis=1)


def _store_rows(ref, val):
    if ref.shape[1] == D_MODEL:
        ref[...] = val
    else:
        n = ref.shape[0] // N_LANE_TILES
        for s in range(N_LANE_TILES):
            ref[pl.ds(s, n, stride=N_LANE_TILES), :] = val[:, s * LANES:(s + 1) * LANES]


def _rows_spec(h, tm):
    if h.shape[1] == D_MODEL:
        return pl.BlockSpec((tm, D_MODEL), lambda i: (i, 0))
    return pl.BlockSpec((tm * N_LANE_TILES, LANES), lambda i: (i, 0))


def _dot_nt(a, b):
    return lax.dot_general(a, b, (((1,), (1,)), ((), ())), preferred_element_type=F32)


def _even_in_kernel(h_ref, nw_ref, w_ref, cw_ref, conv_ref, q_ref, k_ref, v_ref, ubuf,
                    *, tm, tiles_per_seq):
    i = pl.program_id(0)
    xb = _rms(_load_rows(h_ref), nw_ref[...]).astype(BF16)
    c = CONV_WIDTH

    def proj(c0):
        return _dot(xb, w_ref[:, c0:c0 + c])

    gate_b = proj(0)
    u = proj(c) * proj(2 * c)

    @pl.when(i % tiles_per_seq == 0)
    def _():
        ubuf[0:SUBLANES, :] = jnp.zeros((SUBLANES, c), F32)

    ubuf[SUBLANES:SUBLANES + tm, :] = u
    cw = cw_ref[...]
    conv = cw[2:3, :] * u + cw[1:2, :] * ubuf[SUBLANES - 1:SUBLANES - 1 + tm, :] \
        + cw[0:1, :] * ubuf[SUBLANES - 2:SUBLANES - 2 + tm, :]
    conv_ref[...] = (gate_b * conv).astype(BF16)
    ubuf[0:SUBLANES, :] = ubuf[tm:tm + SUBLANES, :]
    q_ref[...] = (proj(3 * c) * (SB_HEAD_DIM ** -0.5)).astype(BF16)
    k_ref[...] = proj(4 * c).astype(BF16)
    v_ref[...] = proj(5 * c).astype(BF16)


def _even_in_proj(h, n_tok, seq, nw, w_in, conv_w):
    tm = ROW_TILE
    c = CONV_WIDTH
    row = lambda i: (i, 0)
    fixed = lambda i: (0, 0)
    out = jax.ShapeDtypeStruct((n_tok, c), BF16)
    return pl.pallas_call(
        functools.partial(_even_in_kernel, tm=tm, tiles_per_seq=seq // tm),
        grid=(n_tok // tm,),
        in_specs=[_rows_spec(h, tm), pl.BlockSpec((1, D_MODEL), fixed),
                  pl.BlockSpec((D_MODEL, 6 * c), fixed), pl.BlockSpec((SHORT_CONV_K, c), fixed)],
        out_specs=[pl.BlockSpec((tm, c), row)] * 4,
        out_shape=[out] * 4,
        scratch_shapes=[pltpu.VMEM((tm + SUBLANES, c), F32)],
        compiler_params=_params("arbitrary"),
        name="even_in_proj",
    )(h, nw, w_in, conv_w)


def _sb_kernel(q_ref, k_ref, v_ref, tri_ref, o_ref, *, seq, bq, group):
    nq = seq // bq
    lane = lax.broadcasted_iota(jnp.int32, (bq, LANES), 1)
    rr = lax.broadcasted_iota(jnp.int32, (bq, bq), 0)
    cc = lax.broadcasted_iota(jnp.int32, (bq, bq), 1)
    causal = cc < rr
    every = cc >= 0
    tri = tri_ref[...]

    def block(qs, kb, vb, run, acc, mask):
        z = _dot_nt(qs, kb)
        sp = jnp.maximum(z, 0.0) + jnp.log(1.0 + jnp.exp(-jnp.abs(z)))
        spm = sp if mask is None else jnp.where(mask, sp, 0.0)
        suffix = _dot(spm.astype(BF16), tri) + run
        p = jnp.exp((z - sp) + suffix)
        if mask is not None:
            p = jnp.where(mask, p, 0.0)
        acc = acc + _dot(p.astype(BF16), vb)
        run = run - jnp.sum(spm, axis=1, keepdims=True)
        return run, acc

    def kv(kj):
        rows = pl.ds(pl.multiple_of(kj * bq, bq), bq)
        return k_ref[rows, :], v_ref[rows, :]

    def heads(qi):
        q = q_ref[qi * bq:(qi + 1) * bq, :]
        return [jnp.where((lane < SB_HEAD_DIM) == (head == 0), q, jnp.zeros_like(q)) for head in range(2)]

    def store(qi, acc):
        o_ref[qi * bq:(qi + 1) * bq, :] = jnp.where(lane < SB_HEAD_DIM, acc[:bq], acc[bq:]).astype(BF16)

    for q0 in range(0, nq, group):
        qhs = [heads(q0 + j) for j in range(group)]
        run = acc = None
        for kk in range(group - 1, -1, -1):
            kb, vb = kv(q0 + kk)
            qs = jnp.concatenate([q for j in range(kk, group) for q in qhs[j]], axis=0)
            mask = jnp.concatenate([causal, causal] + [every] * (2 * (group - 1 - kk)), axis=0)
            run0 = jnp.zeros((2 * bq, 1), F32)
            acc0 = jnp.zeros((2 * bq, LANES), F32)
            run = run0 if run is None else jnp.concatenate([run0, run], axis=0)
            acc = acc0 if acc is None else jnp.concatenate([acc0, acc], axis=0)
            run, acc = block(qs, kb, vb, run, acc, mask)
        if q0 > 0:
            def body(it, cr, qs=qs, q0=q0):
                kb, vb = kv(q0 - 1 - it)
                return block(qs, kb, vb, cr[0], cr[1], None)
            run, acc = lax.fori_loop(0, q0, body, (run, acc))
        for j in range(group):
            store(q0 + j, acc[2 * j * bq:2 * (j + 1) * bq])


def _stick_breaking(q, k, v, n_tok, seq):
    bq = SB_BLOCK
    tri = jnp.asarray(-np.tril(np.ones((bq, bq), np.float32), -1), BF16)
    blk = pl.BlockSpec((seq, LANES), lambda b, hp: (b, hp))
    return pl.pallas_call(
        functools.partial(_sb_kernel, seq=seq, bq=bq, group=SB_GROUP),
        grid=(n_tok // seq, SB_WIDTH // LANES),
        in_specs=[blk, blk, blk, pl.BlockSpec((bq, bq), lambda b, hp: (0, 0))],
        out_specs=blk,
        out_shape=jax.ShapeDtypeStruct((n_tok, SB_WIDTH), BF16),
        compiler_params=_params("arbitrary", "arbitrary"),
        name="stick_breaking",
    )(q, k, v, tri)


def _mix_out_kernel(*refs, n_act, tm):
    h_ref = refs[0]
    act_refs = refs[1:1 + n_act]
    w_refs = refs[1 + n_act:1 + 2 * n_act]
    nw_ref, wrh_ref, wrl_ref, br_ref, triu_ref = refs[1 + 2 * n_act:6 + 2 * n_act]
    hout_ref, ri_ref, rf_ref, cnt_ref, carry = refs[6 + 2 * n_act:]
    i = pl.program_id(0)

    @pl.when(i == 0)
    def _():
        carry[...] = jnp.zeros(carry.shape, F32)

    acc = _load_rows(h_ref)
    for a_ref, w_ref in zip(act_refs, w_refs):
        acc = acc + _dot(a_ref[...], w_ref[...])
    _store_rows(hout_ref, acc)

    xn = _rms(acc, nw_ref[...])
    xh = xn.astype(BF16)
    xl = (xn - xh.astype(F32)).astype(BF16)
    wrh = wrh_ref[...]
    logits = _dot(xh, wrh) + _dot(xl, wrh) + _dot(xh, wrl_ref[...]) + br_ref[...]
    lt = logits.T
    r = lax.broadcasted_iota(jnp.int32, (LANES, tm), 0).astype(F32)
    big = float(LANES)
    neg = -jnp.inf

    def first_argmax(vals):
        m = jnp.max(vals, axis=0, keepdims=True)
        return m, jnp.min(jnp.where(vals == m, r, big), axis=0, keepdims=True)

    gl = jnp.where(r < N_EXPERT_GROUPS, lt, neg)
    gmax, gidx = first_argmax(gl)
    g_top_p = 1.0 / jnp.sum(jnp.exp(gl - gmax), axis=0, keepdims=True)
    lo = N_EXPERT_GROUPS + EXPERTS_PER_GROUP * gidx
    el = jnp.where((r >= lo) & (r < lo + EXPERTS_PER_GROUP), lt, neg)
    m1, i1 = first_argmax(el)
    m2, i2 = first_argmax(jnp.where(r == i1, neg, el))
    t = jnp.exp(m2 - m1)
    g1 = g_top_p / (1.0 + t)
    g2 = g_top_p * t / (1.0 + t)
    e1 = i1 - lo
    e2 = i2 - lo
    ea = jnp.minimum(e1, e2)
    eb = jnp.maximum(e1, e2)
    pair = ea * (2 * EXPERTS_PER_GROUP - 1 - ea) * 0.5 + (eb - ea - 1.0)
    bucket = gidx * N_PAIRS + pair
    first_is_a = e1 < e2
    gate_a = jnp.where(first_is_a, g1, g2)
    gate_b = jnp.where(first_is_a, g2, g1)

    onehot = r == bucket
    cum = _dot(jnp.where(onehot, 1.0, 0.0).astype(BF16), triu_ref[...])
    prev = carry[...]
    rank = jnp.sum(jnp.where(onehot, cum + prev[:, 0:1], 0.0), axis=0, keepdims=True)
    new = prev + jnp.sum(jnp.where(onehot, 1.0, 0.0), axis=1, keepdims=True)
    carry[...] = new
    cnt_ref[...] = new

    row8 = lax.broadcasted_iota(jnp.int32, (SUBLANES, tm), 0)
    ri_ref[...] = jnp.where(row8 == 0, bucket, jnp.where(row8 == 1, rank, 0.0)).astype(jnp.int32)
    rf_ref[...] = jnp.where(row8 == 0, gate_a, jnp.where(row8 == 1, gate_b, 0.0))


def _mix_out_router(h, acts, ws, n_tok, nw, wr_hi, wr_lo, br):
    tm = ROW_TILE
    row = lambda i: (i, 0)
    col = lambda i: (0, i)
    fixed = lambda i: (0, 0)
    triu = jnp.asarray(np.triu(np.ones((tm, tm), np.float32), 1), BF16)
    in_specs = [_rows_spec(h, tm)]
    in_specs += [pl.BlockSpec((tm, a.shape[1]), row) for a in acts]
    in_specs += [pl.BlockSpec(w.shape, fixed) for w in ws]
    in_specs += [pl.BlockSpec((1, D_MODEL), fixed), pl.BlockSpec((D_MODEL, LANES), fixed),
                 pl.BlockSpec((D_MODEL, LANES), fixed), pl.BlockSpec((1, LANES), fixed),
                 pl.BlockSpec((tm, tm), fixed)]
    return pl.pallas_call(
        functools.partial(_mix_out_kernel, n_act=len(acts), tm=tm),
        grid=(n_tok // tm,),
        in_specs=in_specs,
        out_specs=[pl.BlockSpec((tm * N_LANE_TILES, LANES), row), pl.BlockSpec((SUBLANES, tm), col),
                   pl.BlockSpec((SUBLANES, tm), col), pl.BlockSpec((LANES, LANES), fixed)],
        out_shape=[jax.ShapeDtypeStruct((n_tok * N_LANE_TILES, LANES), F32),
                   jax.ShapeDtypeStruct((SUBLANES, n_tok), jnp.int32),
                   jax.ShapeDtypeStruct((SUBLANES, n_tok), F32),
                   jax.ShapeDtypeStruct((LANES, LANES), F32)],
        scratch_shapes=[pltpu.VMEM((LANES, LANES), F32)],
        compiler_params=_params("arbitrary"),
        name="mix_out_router",
    )(h, *acts, *ws, nw, wr_hi, wr_lo, br, triu)


def _ffn_kernel(nused_ref, ea_ref, eb_ref, gcur_ref, gnext_ref, sprev_ref, g_ref, nw_ref,
                wga_ref, wua_ref, wda_ref, wgb_ref, wub_ref, wdb_ref, fw_ref, h_hbm,
                out_hbm, xbuf0, xbuf1, obuf0, obuf1, gsem, ssem, *, bm, n_tok, rpo, final_norm):
    del ea_ref, eb_ref
    i = pl.program_id(0)
    slot = i % 2
    nused = nused_ref[0]
    xbufs = (xbuf0, xbuf1)
    obufs = (obuf0, obuf1)

    rpi = N_LANE_TILES

    def tok_rows(t, n):
        return pl.ds(t * n if isinstance(t, int) else pl.multiple_of(t * n, n), n)

    def row_in(gref, s, r):
        return pltpu.make_async_copy(h_hbm.at[tok_rows(gref[0, 0, r], rpi)], xbufs[s].at[tok_rows(r, rpi)],
                                     gsem.at[s])

    def row_out(s, r):
        return pltpu.make_async_copy(obufs[s].at[tok_rows(r, rpo)], out_hbm.at[tok_rows(sprev_ref[0, 0, r], rpo)],
                                     ssem.at[s])

    def wait_gather(s):
        pltpu.make_async_copy(h_hbm.at[pl.ds(0, bm * rpi)], xbufs[s], gsem.at[s]).wait()

    def wait_scatter(s):
        pltpu.make_async_copy(obufs[s], out_hbm.at[pl.ds(0, bm * rpo)], ssem.at[s]).wait()

    @pl.when(i == 0)
    def _():
        def body(r, c):
            row_in(gcur_ref, 0, r).start()
            return c
        lax.fori_loop(0, bm, body, 0, unroll=8)
        for s in range(2):
            obufs[s][...] = jnp.zeros(obufs[s].shape, F32)
            cp = pltpu.make_async_copy(obufs[s], out_hbm.at[pl.ds((n_tok + s * bm) * rpo, bm * rpo)], ssem.at[s])
            cp.start()
            if s == 1:
                cp.wait()

    def run_block(cur, nxt):
        for r in range(bm):
            row_in(gnext_ref, nxt, r).start(priority=r % 2)
        wait_gather(cur)
        x = _load_rows(xbufs[cur])
        xb = _rms(x, nw_ref[...]).astype(BF16)
        g = g_ref[...]

        def hidden(wg_ref, wu_ref, gate):
            return (jax.nn.silu(_dot(xb, wg_ref[0, 0].astype(BF16))) * _dot(xb, wu_ref[0, 0].astype(BF16))
                    * gate).astype(BF16)

        hid_a = hidden(wga_ref, wua_ref, g[:, 0:1])
        hid_b = hidden(wgb_ref, wub_ref, g[:, 1:2])
        for r in range(bm):
            row_out(nxt, r).start(priority=r % 2)
        wait_scatter(cur)
        y = x + _dot(hid_a, wda_ref[0, 0].astype(BF16)) + _dot(hid_b, wdb_ref[0, 0].astype(BF16))
        if final_norm:
            y = _rms(y, fw_ref[...])
        _store_rows(obufs[cur], y)

    for s in range(2):
        @pl.when((i < nused) & (slot == s))
        def _(s=s):
            run_block(s, 1 - s)

    @pl.when(i == nused)
    def _():
        for s in range(2):
            @pl.when(slot == s)
            def _(s=s):
                wait_gather(s)
                wait_scatter(s)

                def body(r, c):
                    row_out(1 - s, r).start()
                    return c
                lax.fori_loop(0, bm, body, 0, unroll=8)
                wait_scatter(1 - s)


def _moe_ffn(h, n_tok, nused, blk_ea, blk_eb, gather_rows, scatter_rows, gate_rows, nw, layer, wg, wu, wd, fw,
             final_norm):
    bm = MOE_BLOCK
    n_blocks = gather_rows.shape[0]
    rpo = 1 if final_norm else N_LANE_TILES
    out_cols = D_MODEL // rpo
    cur = lambda i, nu, ea, eb: (i, 0, 0)
    nxt = lambda i, nu, ea, eb: (jnp.minimum(i + 1, n_blocks - 1), 0, 0)
    idx_spec = lambda imap: pl.BlockSpec((1, 1, bm), imap, memory_space=pltpu.SMEM)
    fixed = lambda i, nu, ea, eb: (0, 0)
    wa = lambda i, nu, ea, eb: (layer, ea[i], 0, 0)
    wb = lambda i, nu, ea, eb: (layer, eb[i], 0, 0)
    up = pl.BlockSpec((1, 1, D_MODEL, EXPERT_FF), wa)
    upb = pl.BlockSpec((1, 1, D_MODEL, EXPERT_FF), wb)
    grid_spec = pltpu.PrefetchScalarGridSpec(
        num_scalar_prefetch=3,
        grid=(n_blocks,),
        in_specs=[idx_spec(cur), idx_spec(nxt), idx_spec(cur),
                  pl.BlockSpec((bm, 2), lambda i, nu, ea, eb: (i, 0)),
                  pl.BlockSpec((1, D_MODEL), fixed),
                  up, up, pl.BlockSpec((1, 1, EXPERT_FF, D_MODEL), wa),
                  upb, upb, pl.BlockSpec((1, 1, EXPERT_FF, D_MODEL), wb),
                  pl.BlockSpec((1, D_MODEL), fixed),
                  pl.BlockSpec(memory_space=pl.ANY)],
        out_specs=pl.BlockSpec(memory_space=pl.ANY),
        scratch_shapes=[pltpu.VMEM((bm * N_LANE_TILES, LANES), F32)] * 2 + [pltpu.VMEM((bm * rpo, out_cols), F32)] * 2
        + [pltpu.SemaphoreType.DMA((2,)), pltpu.SemaphoreType.DMA((2,))],
    )
    return pl.pallas_call(
        functools.partial(_ffn_kernel, bm=bm, n_tok=n_tok, rpo=rpo, final_norm=final_norm),
        grid_spec=grid_spec,
        out_shape=jax.ShapeDtypeStruct(((n_tok + 2 * bm) * rpo, out_cols), F32),
        compiler_params=_params("arbitrary"),
        name="moe_ffn",
    )(nused, blk_ea, blk_eb, gather_rows, gather_rows, scatter_rows, gate_rows, nw, wg, wu, wd, wg, wu, wd,
      fw, h)


_PAIR_A = np.array([a for a in range(EXPERTS_PER_GROUP) for b in range(a + 1, EXPERTS_PER_GROUP)], np.int32)
_PAIR_B = np.array([b for a in range(EXPERTS_PER_GROUP) for b in range(a + 1, EXPERTS_PER_GROUP)], np.int32)


def _moe_layer(h, n_tok, route_i, route_f, counts, nw, layer, wg, wu, wd, fw, final_norm):
    bm = MOE_BLOCK
    n_blocks = n_tok // bm + N_BUCKETS + 1
    bucket = route_i[0]
    rank = route_i[1]
    cnt = counts[:N_BUCKETS, 0].astype(jnp.int32)
    nblk = (cnt + bm - 1) // bm
    blk_end = jnp.cumsum(nblk)
    blk_start = blk_end - nblk
    in_bucket = bucket[:, None] == jnp.arange(N_BUCKETS, dtype=jnp.int32)[None, :]
    dest = jnp.sum(jnp.where(in_bucket, blk_start[None, :], 0), axis=1) * bm + rank
    upd = jnp.stack([jnp.arange(n_tok, dtype=jnp.int32), lax.bitcast_convert_type(route_f[0], jnp.int32),
                     lax.bitcast_convert_type(route_f[1], jnp.int32)], axis=1)
    init = jnp.broadcast_to(jnp.array([n_tok, 0, 0], jnp.int32), (n_blocks * bm, 3))
    packed = init.at[dest].set(upd, unique_indices=True)
    tok_rows = packed[:, 0]
    gate_rows = lax.bitcast_convert_type(packed[:, 1:3], F32)
    blk_bucket = jnp.minimum(
        jnp.searchsorted(blk_end, jnp.arange(n_blocks, dtype=jnp.int32), side='right'),
        N_BUCKETS - 1).astype(jnp.int32)
    nused = blk_end[-1:].astype(jnp.int32)
    last = blk_bucket[jnp.maximum(nused[0] - 1, 0)]
    blk_bucket = jnp.where(jnp.arange(n_blocks) < nused[0], blk_bucket, last)
    grp = blk_bucket // N_PAIRS
    pair = blk_bucket % N_PAIRS
    blk_ea = grp * EXPERTS_PER_GROUP + jnp.asarray(_PAIR_A)[pair]
    blk_eb = grp * EXPERTS_PER_GROUP + jnp.asarray(_PAIR_B)[pair]
    row = jnp.arange(n_blocks * bm, dtype=jnp.int32)
    gather_rows = jnp.minimum(tok_rows, n_tok - 1)
    scatter_rows = jnp.where(tok_rows >= n_tok, n_tok + (row // bm) % 2 * bm + row % bm, tok_rows)
    scatter_rows = jnp.concatenate([n_tok + bm + row[:bm], scatter_rows])
    return _moe_ffn(h, n_tok, nused, blk_ea, blk_eb, gather_rows.reshape(n_blocks, 1, bm),
                    scatter_rows.reshape(n_blocks + 1, 1, bm), gate_rows, nw, layer, wg, wu, wd, fw,
                    final_norm)


def _odd_in_kernel(h_ref, nw_ref, wz_ref, wx_ref, wdt_ref, wdtt_ref, cw_ref, cb_ref, dtb_ref, dtbt_ref,
                   z_ref, x_ref, b_ref, c_ref, dt_ref, dtt_ref, ubuf, *, tm, tiles_per_seq):
    i = pl.program_id(0)
    xb = _rms(_load_rows(h_ref), nw_ref[...]).astype(BF16)
    cw_all = cw_ref[...]
    cb_all = cb_ref[...]
    wchunk = 512

    @pl.when(i % tiles_per_seq == 0)
    def _():
        ubuf[0:SUBLANES, :] = jnp.zeros((SUBLANES, SSM_CONV_DIM), F32)

    for c0 in range(0, D_INNER, wchunk):
        z_ref[:, c0:c0 + wchunk] = _dot(xb, wz_ref[:, c0:c0 + wchunk]).astype(BF16)

    for c0 in range(0, SSM_CONV_DIM, wchunk):
        cols = slice(c0, c0 + wchunk)
        u = _dot(xb, wx_ref[:, cols])
        ubuf[SUBLANES:SUBLANES + tm, cols] = u
        cw = cw_all[:, cols]
        conv = cw[3:4, :] * u + cb_all[:, cols]
        for j in range(1, SSM_CONV_K):
            conv = conv + cw[3 - j:4 - j, :] * ubuf[SUBLANES - j:SUBLANES - j + tm, cols]
        act = jax.nn.silu(conv).astype(BF16)
        if c0 < D_INNER:
            x_ref[:, c0:c0 + wchunk] = act
        elif c0 < D_INNER + SSM_GN:
            b_ref[:, c0 - D_INNER:c0 - D_INNER + wchunk] = act
        else:
            c_ref[:, c0 - D_INNER - SSM_GN:c0 - D_INNER - SSM_GN + wchunk] = act
    ubuf[0:SUBLANES, :] = ubuf[tm:tm + SUBLANES, :]

    dt_ref[...] = jax.nn.softplus(_dot(xb, wdt_ref[...]) + dtb_ref[...])
    dtt_ref[...] = jax.nn.softplus(_dot_nt(wdtt_ref[...], xb) + dtbt_ref[...])


def _odd_in_proj(h, n_tok, seq, nw, wz, wx, wdt, wdtt, conv_w, conv_b, dtb, dtbt):
    tm = ROW_TILE
    row = lambda i: (i, 0)
    col = lambda i: (0, i)
    fixed = lambda i: (0, 0)
    full = lambda a: pl.BlockSpec(a.shape, fixed)
    return pl.pallas_call(
        functools.partial(_odd_in_kernel, tm=tm, tiles_per_seq=seq // tm),
        grid=(n_tok // tm,),
        in_specs=[_rows_spec(h, tm), full(nw), full(wz), full(wx), full(wdt), full(wdtt),
                  full(conv_w), full(conv_b), full(dtb), full(dtbt)],
        out_specs=[pl.BlockSpec((tm, D_INNER), row), pl.BlockSpec((tm, D_INNER), row),
                   pl.BlockSpec((tm, SSM_GN), row), pl.BlockSpec((tm, SSM_GN), row),
                   pl.BlockSpec((tm, SSM_HEADS), row), pl.BlockSpec((SSM_HEADS, tm), col)],
        out_shape=[jax.ShapeDtypeStruct((n_tok, D_INNER), BF16), jax.ShapeDtypeStruct((n_tok, D_INNER), BF16),
                   jax.ShapeDtypeStruct((n_tok, SSM_GN), BF16), jax.ShapeDtypeStruct((n_tok, SSM_GN), BF16),
                   jax.ShapeDtypeStruct((n_tok, SSM_HEADS), F32), jax.ShapeDtypeStruct((SSM_HEADS, n_tok), F32)],
        scratch_shapes=[pltpu.VMEM((tm + SUBLANES, SSM_CONV_DIM), F32)],
        compiler_params=_params("arbitrary"),
        name="odd_in_proj",
    )(h, nw, wz, wx, wdt, wdtt, conv_w, conv_b, dtb, dtbt)


def _ssd_kernel(x_ref, b_ref, c_ref, z_ref, dt_ref, dtt_ref, alog_ref, alogt_ref, dskip_ref, normw_ref,
                tril_ref, triu_ref, spread_ref, y_ref, state, *, lc):
    ci = pl.program_id(1)

    @pl.when(ci == 0)
    def _():
        state[...] = jnp.zeros(state.shape, F32)

    gw = SSM_HPG * SSM_HEAD_DIM
    hi = lax.Precision.HIGHEST
    dt = dt_ref[...]
    dtt = dtt_ref[...]
    a = -jnp.exp(alog_ref[...])
    at = -jnp.exp(alogt_ref[...])
    cs = jnp.dot(tril_ref[...], dt * a, precision=hi, preferred_element_type=F32)
    cst = jnp.dot(dtt * at, triu_ref[...], precision=hi, preferred_element_type=F32)
    cs_last = cs[lc - 1:lc, :]
    grow = jnp.exp(cs)
    dt_end = dt * jnp.exp(cs_last - cs)
    e_last = jnp.exp(cs_last)
    rr = lax.broadcasted_iota(jnp.int32, (lc, lc), 0)
    cc = lax.broadcasted_iota(jnp.int32, (lc, lc), 1)
    lower = cc <= rr
    lane = lax.broadcasted_iota(jnp.int32, (lc, gw), 1)
    head_of_lane = lane // SSM_HEAD_DIM
    fac = jnp.concatenate([dt, grow, dt_end, jnp.broadcast_to(e_last, (SUBLANES, SSM_HEADS))], axis=0)
    fac_hi = fac.astype(BF16).astype(F32)
    spread = _dot(jnp.concatenate([fac_hi, fac - fac_hi], axis=1).astype(BF16), spread_ref[...])

    for g in range(SSM_GROUPS):
        h0 = g * SSM_HPG
        bg = b_ref[:, g * SSM_STATE:(g + 1) * SSM_STATE]
        cg = c_ref[:, g * SSM_STATE:(g + 1) * SSM_STATE]
        xg = x_ref[:, g * gw:(g + 1) * gw].astype(F32)
        cb = _dot_nt(cg, bg)
        cols = slice(g * gw, (g + 1) * gw)
        xdt = xg * spread[0:lc, cols]
        xdt_b = xdt.astype(BF16)
        st = state[g]
        y = _dot(cg, st.astype(BF16)) * spread[lc:2 * lc, cols]
        for k in range(SSM_HPG):
            h = h0 + k
            seg = cs[:, h:h + 1] - cst[h:h + 1, :]
            decay = jnp.exp(jnp.where(lower, seg, -jnp.inf))
            m = (cb * decay).astype(BF16)
            y = y + _dot(m, jnp.where(head_of_lane == k, xdt_b, jnp.zeros_like(xdt_b)))
        x_end = (xg * spread[2 * lc:3 * lc, cols]).astype(BF16)
        bgt = bg.astype(F32).T.astype(BF16)
        state[g] = st * spread[3 * lc:3 * lc + 1, cols] + _dot(bgt, x_end)
        y = y + dskip_ref[:, g * gw:(g + 1) * gw] * xg
        y = y * jax.nn.silu(z_ref[:, g * gw:(g + 1) * gw].astype(F32))
        y = y * lax.rsqrt(jnp.mean(y * y, axis=-1, keepdims=True) + RMS_EPS)
        y_ref[:, g * gw:(g + 1) * gw] = (y * normw_ref[:, g * gw:(g + 1) * gw]).astype(BF16)


def _ssd(x, bm, cm, z, dt, dtt, n_tok, seq, alog, alogt, dskip, normw):
    lc = SSD_CHUNK
    nc = seq // lc
    tril = jnp.asarray(np.tril(np.ones((lc, lc), np.float32)))
    triu = jnp.asarray(np.triu(np.ones((lc, lc), np.float32)))
    head_of_col = np.arange(D_INNER) // SSM_HEAD_DIM
    spread = jnp.asarray(np.arange(2 * SSM_HEADS)[:, None] % SSM_HEADS == head_of_col[None, :], BF16)
    row = lambda b, c: (b * nc + c, 0)
    col = lambda b, c: (0, b * nc + c)
    fixed = lambda b, c: (0, 0)
    full = lambda a: pl.BlockSpec(a.shape, fixed)
    return pl.pallas_call(
        functools.partial(_ssd_kernel, lc=lc),
        grid=(n_tok // seq, nc),
        in_specs=[pl.BlockSpec((lc, D_INNER), row), pl.BlockSpec((lc, SSM_GN), row),
                  pl.BlockSpec((lc, SSM_GN), row), pl.BlockSpec((lc, D_INNER), row),
                  pl.BlockSpec((lc, SSM_HEADS), row), pl.BlockSpec((SSM_HEADS, lc), col),
                  full(alog), full(alogt), full(dskip), full(normw), full(tril), full(triu),
                  full(spread)],
        out_specs=pl.BlockSpec((lc, D_INNER), row),
        out_shape=jax.ShapeDtypeStruct((n_tok, D_INNER), BF16),
        scratch_shapes=[pltpu.VMEM((SSM_GROUPS, SSM_STATE, SSM_HPG * SSM_HEAD_DIM), F32)],
        compiler_params=_params("arbitrary", "arbitrary"),
        name="ssd_scan",
    )(x, bm, cm, z, dt, dtt, alog, alogt, dskip, normw, tril, triu, spread)


def _router_weights(wg, bg, we, be):
    w = jnp.concatenate([wg, jnp.moveaxis(we, 0, 1).reshape(D_MODEL, -1)], axis=1)
    b = jnp.concatenate([bg, be.reshape(-1)])
    pad = LANES - w.shape[1]
    w = jnp.pad(w, ((0, 0), (0, pad)))
    b = jnp.pad(b, (0, pad)).reshape(1, LANES)
    w_hi = w.astype(BF16)
    w_lo = (w - w_hi.astype(F32)).astype(BF16)
    return w_hi, w_lo, b


def kernel(x, even_w_in, even_conv_w, even_w_out, odd_in_proj, odd_conv_w, odd_conv_b, odd_dt_bias,
           odd_a_log, odd_d_skip, odd_norm_w, odd_out_proj, norm_mix, norm_ffn, router_g_w, router_g_b,
           router_e_w, router_e_b, expert_w_gate, expert_w_up, expert_w_down, final_norm):
    bsz, seq, d = x.shape
    n_tok = bsz * seq
    depth = norm_mix.shape[0]
    h = x.reshape(n_tok, d)
    fw = final_norm.reshape(1, d)
    for layer in range(depth):
        i = layer // 2
        nw = norm_mix[layer].reshape(1, d)
        if layer % 2 == 0:
            conv, q, k, v = _even_in_proj(h, n_tok, seq, nw, even_w_in[i].astype(BF16), even_conv_w[i])
            sb = _stick_breaking(q, k, v, n_tok, seq)
            w_out = even_w_out[i].astype(BF16)
            acts, ws = [conv, sb], [w_out[:CONV_WIDTH], w_out[CONV_WIDTH:]]
        else:
            w_in = odd_in_proj[i]
            wz = w_in[:, :D_INNER].astype(BF16)
            wx = w_in[:, D_INNER:D_INNER + SSM_CONV_DIM].astype(BF16)
            wdt = w_in[:, D_INNER + SSM_CONV_DIM:].astype(BF16)
            z, xs, bmat, cmat, dt, dtt = _odd_in_proj(
                h, n_tok, seq, nw, wz, wx, wdt, wdt.T, odd_conv_w[i], odd_conv_b[i].reshape(1, -1),
                odd_dt_bias[i].reshape(1, -1), odd_dt_bias[i].reshape(-1, 1))
            y = _ssd(xs, bmat, cmat, z, dt, dtt, n_tok, seq, odd_a_log[i].reshape(1, -1),
                     odd_a_log[i].reshape(-1, 1), jnp.repeat(odd_d_skip[i], SSM_HEAD_DIM).reshape(1, -1),
                     odd_norm_w[i].reshape(1, -1))
            acts, ws = [y], [odd_out_proj[i].astype(BF16)]
        wr_hi, wr_lo, br = _router_weights(router_g_w[layer], router_g_b[layer], router_e_w[layer],
                                           router_e_b[layer])
        h, route_i, route_f, counts = _mix_out_router(h, acts, ws, n_tok, norm_ffn[layer].reshape(1, d),
                                                      wr_hi, wr_lo, br)
        h = _moe_layer(h, n_tok, route_i, route_f, counts, norm_ffn[layer].reshape(1, d),
                       layer, expert_w_gate, expert_w_up, expert_w_down, fw, final_norm=(layer == depth - 1))
    return h[:n_tok].reshape(bsz, seq, d)
```

```python
import functools

import numpy as np
import jax
import jax.numpy as jnp
from jax import lax
from jax.experimental import pallas as pl
from jax.experimental.pallas import tpu as pltpu

F32 = jnp.float32
BF16 = jnp.bfloat16

D_MODEL = 1024
RMS_EPS = 1e-6
CONV_WIDTH = 512
SHORT_CONV_K = 3
SB_HEAD_DIM = 64
SB_WIDTH = 512
D_INNER = 2048
SSM_HEAD_DIM = 64
SSM_HEADS = 32
SSM_GROUPS = 8
SSM_HPG = 4
SSM_STATE = 128
SSM_CONV_K = 4
SSM_GN = SSM_GROUPS * SSM_STATE
SSM_CONV_DIM = D_INNER + 2 * SSM_GN
N_EXPERT_GROUPS = 4
EXPERTS_PER_GROUP = 8
EXPERT_FF = 512
N_PAIRS = EXPERTS_PER_GROUP * (EXPERTS_PER_GROUP - 1) // 2
N_BUCKETS = N_EXPERT_GROUPS * N_PAIRS

LANES = 128
SUBLANES = 8
N_LANE_TILES = D_MODEL // LANES
VMEM_LIMIT = 56 * 1024 * 1024

ROW_TILE = 512
SB_BLOCK = 256
SB_GROUP = 8
SSD_CHUNK = 128
MOE_BLOCK = 320


def _params(*sem):
    return pltpu.CompilerParams(dimension_semantics=sem, vmem_limit_bytes=VMEM_LIMIT)


def _rms(x, w):
    return x * lax.rsqrt(jnp.mean(x * x, axis=-1, keepdims=True) + RMS_EPS) * w


def _dot(a, b):
    return jnp.dot(a, b, preferred_element_type=F32)


def _load_rows(ref):
    if ref.shape[1] == D_MODEL:
        return ref[...]
    n = ref.shape[0] // N_LANE_TILES
    return jnp.concatenate([ref[pl.ds(s, n, stride=N_LANE_TILES), :] for s in range(N_LANE_TILES)], axis=1)


def _store_rows(ref, val):
    if ref.shape[1] == D_MODEL:
        ref[...] = val
    else:
        n = ref.shape[0] // N_LANE_TILES
        for s in range(N_LANE_TILES):
            ref[pl.ds(s, n, stride=N_LANE_TILES), :] = val[:, s * LANES:(s + 1) * LANES]


def _rows_spec(h, tm):
    if h.shape[1] == D_MODEL:
        return pl.BlockSpec((tm, D_MODEL), lambda i: (i, 0))
    return pl.BlockSpec((tm * N_LANE_TILES, LANES), lambda i: (i, 0))


def _dot_nt(a, b):
    return lax.dot_general(a, b, (((1,), (1,)), ((), ())), preferred_element_type=F32)


def _even_in_kernel(h_ref, nw_ref, w_ref, cw_ref, conv_ref, q_ref, k_ref, v_ref, ubuf,
                    *, tm, tiles_per_seq):
    i = pl.program_id(0)
    xb = _rms(_load_rows(h_ref), nw_ref[...]).astype(BF16)
    c = CONV_WIDTH

    def proj(c0):
        return _dot(xb, w_ref[:, c0:c0 + c])

    gate_b = proj(0)
    u = proj(c) * proj(2 * c)

    @pl.when(i % tiles_per_seq == 0)
    def _():
        ubuf[0:SUBLANES, :] = jnp.zeros((SUBLANES, c), F32)

    ubuf[SUBLANES:SUBLANES + tm, :] = u
    cw = cw_ref[...]
    conv = cw[2:3, :] * u + cw[1:2, :] * ubuf[SUBLANES - 1:SUBLANES - 1 + tm, :] \
        + cw[0:1, :] * ubuf[SUBLANES - 2:SUBLANES - 2 + tm, :]
    conv_ref[...] = (gate_b * conv).astype(BF16)
    ubuf[0:SUBLANES, :] = ubuf[tm:tm + SUBLANES, :]
    q_ref[...] = (proj(3 * c) * (SB_HEAD_DIM ** -0.5)).astype(BF16)
    k_ref[...] = proj(4 * c).astype(BF16)
    v_ref[...] = proj(5 * c).astype(BF16)


def _even_in_proj(h, n_tok, seq, nw, w_in, conv_w):
    tm = ROW_TILE
    c = CONV_WIDTH
    row = lambda i: (i, 0)
    fixed = lambda i: (0, 0)
    out = jax.ShapeDtypeStruct((n_tok, c), BF16)
    return pl.pallas_call(
        functools.partial(_even_in_kernel, tm=tm, tiles_per_seq=seq // tm),
        grid=(n_tok // tm,),
        in_specs=[_rows_spec(h, tm), pl.BlockSpec((1, D_MODEL), fixed),
                  pl.BlockSpec((D_MODEL, 6 * c), fixed), pl.BlockSpec((SHORT_CONV_K, c), fixed)],
        out_specs=[pl.BlockSpec((tm, c), row)] * 4,
        out_shape=[out] * 4,
        scratch_shapes=[pltpu.VMEM((tm + SUBLANES, c), F32)],
        compiler_params=_params("arbitrary"),
        name="even_in_proj",
    )(h, nw, w_in, conv_w)


def _sb_kernel(q_ref, k_ref, v_ref, tri_ref, o_ref, *, seq, bq, group):
    nq = seq // bq
    lane = lax.broadcasted_iota(jnp.int32, (bq, LANES), 1)
    rr = lax.broadcasted_iota(jnp.int32, (bq, bq), 0)
    cc = lax.broadcasted_iota(jnp.int32, (bq, bq), 1)
    causal = cc < rr
    every = cc >= 0
    tri = tri_ref[...]

    def block(qs, kb, vb, run, acc, mask):
        z = _dot_nt(qs, kb)
        sp = jnp.maximum(z, 0.0) + jnp.log(1.0 + jnp.exp(-jnp.abs(z)))
        spm = sp if mask is None else jnp.where(mask, sp, 0.0)
        suffix = _dot(spm.astype(BF16), tri) + run
        p = jnp.exp((z - sp) + suffix)
        if mask is not None:
            p = jnp.where(mask, p, 0.0)
        acc = acc + _dot(p.astype(BF16), vb)
        run = run - jnp.sum(spm, axis=1, keepdims=True)
        return run, acc

    def kv(kj):
        rows = pl.ds(pl.multiple_of(kj * bq, bq), bq)
        return k_ref[rows, :], v_ref[rows, :]

    def heads(qi):
        q = q_ref[qi * bq:(qi + 1) * bq, :]
        return [jnp.where((lane < SB_HEAD_DIM) == (head == 0), q, jnp.zeros_like(q)) for head in range(2)]

    def store(qi, acc):
        o_ref[qi * bq:(qi + 1) * bq, :] = jnp.where(lane < SB_HEAD_DIM, acc[:bq], acc[bq:]).astype(BF16)

    for q0 in range(0, nq, group):
        qhs = [heads(q0 + j) for j in range(group)]
        run = acc = None
        for kk in range(group - 1, -1, -1):
            kb, vb = kv(q0 + kk)
            qs = jnp.concatenate([q for j in range(kk, group) for q in qhs[j]], axis=0)
            mask = jnp.concatenate([causal, causal] + [every] * (2 * (group - 1 - kk)), axis=0)
            run0 = jnp.zeros((2 * bq, 1), F32)
            acc0 = jnp.zeros((2 * bq, LANES), F32)
            run = run0 if run is None else jnp.concatenate([run0, run], axis=0)
            acc = acc0 if acc is None else jnp.concatenate([acc0, acc], axis=0)
            run, acc = block(qs, kb, vb, run, acc, mask)
        if q0 > 0:
            def body(it, cr, qs=qs, q0=q0):
                kb, vb = kv(q0 - 1 - it)
                return block(qs, kb, vb, cr[0], cr[1], None)
            run, acc = lax.fori_loop(0, q0, body, (run, acc))
        for j in range(group):
            store(q0 + j, acc[2 * j * bq:2 * (j + 1) * bq])


def _stick_breaking(q, k, v, n_tok, seq):
    bq = SB_BLOCK
    tri = jnp.asarray(-np.tril(np.ones((bq, bq), np.float32), -1), BF16)
    blk = pl.BlockSpec((seq, LANES), lambda b, hp: (b, hp))
    return pl.pallas_call(
        functools.partial(_sb_kernel, seq=seq, bq=bq, group=SB_GROUP),
        grid=(n_tok // seq, SB_WIDTH // LANES),
        in_specs=[blk, blk, blk, pl.BlockSpec((bq, bq), lambda b, hp: (0, 0))],
        out_specs=blk,
        out_shape=jax.ShapeDtypeStruct((n_tok, SB_WIDTH), BF16),
        compiler_params=_params("arbitrary", "arbitrary"),
        name="stick_breaking",
    )(q, k, v, tri)


def _mix_out_kernel(*refs, n_act, tm):
    h_ref = refs[0]
    act_refs = refs[1:1 + n_act]
    w_refs = refs[1 + n_act:1 + 2 * n_act]
    nw_ref, wrh_ref, wrl_ref, br_ref, triu_ref = refs[1 + 2 * n_act:6 + 2 * n_act]
    hout_ref, ri_ref, rf_ref, cnt_ref, carry = refs[6 + 2 * n_act:]
    i = pl.program_id(0)

    @pl.when(i == 0)
    def _():
        carry[...] = jnp.zeros(carry.shape, F32)

    acc = _load_rows(h_ref)
    for a_ref, w_ref in zip(act_refs, w_refs):
        acc = acc + _dot(a_ref[...], w_ref[...])
    _store_rows(hout_ref, acc)

    xn = _rms(acc, nw_ref[...])
    xh = xn.astype(BF16)
    xl = (xn - xh.astype(F32)).astype(BF16)
    wrh = wrh_ref[...]
    logits = _dot(xh, wrh) + _dot(xl, wrh) + _dot(xh, wrl_ref[...]) + br_ref[...]
    lt = logits.T
    r = lax.broadcasted_iota(jnp.int32, (LANES, tm), 0).astype(F32)
    big = float(LANES)
    neg = -jnp.inf

    def first_argmax(vals):
        m = jnp.max(vals, axis=0, keepdims=True)
        return m, jnp.min(jnp.where(vals == m, r, big), axis=0, keepdims=True)

    gl = jnp.where(r < N_EXPERT_GROUPS, lt, neg)
    gmax, gidx = first_argmax(gl)
    g_top_p = 1.0 / jnp.sum(jnp.exp(gl - gmax), axis=0, keepdims=True)
    lo = N_EXPERT_GROUPS + EXPERTS_PER_GROUP * gidx
    el = jnp.where((r >= lo) & (r < lo + EXPERTS_PER_GROUP), lt, neg)
    m1, i1 = first_argmax(el)
    m2, i2 = first_argmax(jnp.where(r == i1, neg, el))
    t = jnp.exp(m2 - m1)
    g1 = g_top_p / (1.0 + t)
    g2 = g_top_p * t / (1.0 + t)
    e1 = i1 - lo
    e2 = i2 - lo
    ea = jnp.minimum(e1, e2)
    eb = jnp.maximum(e1, e2)
    pair = ea * (2 * EXPERTS_PER_GROUP - 1 - ea) * 0.5 + (eb - ea - 1.0)
    bucket = gidx * N_PAIRS + pair
    first_is_a = e1 < e2
    gate_a = jnp.where(first_is_a, g1, g2)
    gate_b = jnp.where(first_is_a, g2, g1)

    onehot = r == bucket
    cum = _dot(jnp.where(onehot, 1.0, 0.0).astype(BF16), triu_ref[...])
    prev = carry[...]
    rank = jnp.sum(jnp.where(onehot, cum + prev[:, 0:1], 0.0), axis=0, keepdims=True)
    new = prev + jnp.sum(jnp.where(onehot, 1.0, 0.0), axis=1, keepdims=True)
    carry[...] = new
    cnt_ref[...] = new

    row8 = lax.broadcasted_iota(jnp.int32, (SUBLANES, tm), 0)
    ri_ref[...] = jnp.where(row8 == 0, bucket, jnp.where(row8 == 1, rank, 0.0)).astype(jnp.int32)
    rf_ref[...] = jnp.where(row8 == 0, gate_a, jnp.where(row8 == 1, gate_b, 0.0))


def _mix_out_router(h, acts, ws, n_tok, nw, wr_hi, wr_lo, br):
    tm = ROW_TILE
    row = lambda i: (i, 0)
    col = lambda i: (0, i)
    fixed = lambda i: (0, 0)
    triu = jnp.asarray(np.triu(np.ones((tm, tm), np.float32), 1), BF16)
    in_specs = [_rows_spec(h, tm)]
    in_specs += [pl.BlockSpec((tm, a.shape[1]), row) for a in acts]
    in_specs += [pl.BlockSpec(w.shape, fixed) for w in ws]
    in_specs += [pl.BlockSpec((1, D_MODEL), fixed), pl.BlockSpec((D_MODEL, LANES), fixed),
                 pl.BlockSpec((D_MODEL, LANES), fixed), pl.BlockSpec((1, LANES), fixed),
                 pl.BlockSpec((tm, tm), fixed)]
    return pl.pallas_call(
        functools.partial(_mix_out_kernel, n_act=len(acts), tm=tm),
        grid=(n_tok // tm,),
        in_specs=in_specs,
        out_specs=[pl.BlockSpec((tm * N_LANE_TILES, LANES), row), pl.BlockSpec((SUBLANES, tm), col),
                   pl.BlockSpec((SUBLANES, tm), col), pl.BlockSpec((LANES, LANES), fixed)],
        out_shape=[jax.ShapeDtypeStruct((n_tok * N_LANE_TILES, LANES), F32),
                   jax.ShapeDtypeStruct((SUBLANES, n_tok), jnp.int32),
                   jax.ShapeDtypeStruct((SUBLANES, n_tok), F32),
                   jax.ShapeDtypeStruct((LANES, LANES), F32)],
        scratch_shapes=[pltpu.VMEM((LANES, LANES), F32)],
        compiler_params=_params("arbitrary"),
        name="mix_out_router",
    )(h, *acts, *ws, nw, wr_hi, wr_lo, br, triu)


def _ffn_kernel(nused_ref, ea_ref, eb_ref, gcur_ref, gnext_ref, sprev_ref, g_ref, nw_ref,
                wga_ref, wua_ref, wda_ref, wgb_ref, wub_ref, wdb_ref, fw_ref, h_hbm,
                out_hbm, xbuf0, xbuf1, obuf0, obuf1, gsem, ssem, *, bm, n_tok, rpo, final_norm):
    del ea_ref, eb_ref
    i = pl.program_id(0)
    slot = i % 2
    nused = nused_ref[0]
    xbufs = (xbuf0, xbuf1)
    obufs = (obuf0, obuf1)

    rpi = N_LANE_TILES

    def tok_rows(t, n):
        return pl.ds(t * n if isinstance(t, int) else pl.multiple_of(t * n, n), n)

    def row_in(gref, s, r):
        return pltpu.make_async_copy(h_hbm.at[tok_rows(gref[0, 0, r], rpi)], xbufs[s].at[tok_rows(r, rpi)],
                                     gsem.at[s])

    def row_out(s, r):
        return pltpu.make_async_copy(obufs[s].at[tok_rows(r, rpo)], out_hbm.at[tok_rows(sprev_ref[0, 0, r], rpo)],
                                     ssem.at[s])

    def wait_gather(s):
        pltpu.make_async_copy(h_hbm.at[pl.ds(0, bm * rpi)], xbufs[s], gsem.at[s]).wait()

    def wait_scatter(s):
        pltpu.make_async_copy(obufs[s], out_hbm.at[pl.ds(0, bm * rpo)], ssem.at[s]).wait()

    @pl.when(i == 0)
    def _():
        def body(r, c):
            row_in(gcur_ref, 0, r).start()
            return c
        lax.fori_loop(0, bm, body, 0, unroll=8)
        for s in range(2):
            obufs[s][...] = jnp.zeros(obufs[s].shape, F32)
            cp = pltpu.make_async_copy(obufs[s], out_hbm.at[pl.ds((n_tok + s * bm) * rpo, bm * rpo)], ssem.at[s])
            cp.start()
            if s == 1:
                cp.wait()

    def run_block(cur, nxt):
        for r in range(bm):
            row_in(gnext_ref, nxt, r).start(priority=r % 2)
        wait_gather(cur)
        x = _load_rows(xbufs[cur])
        xb = _rms(x, nw_ref[...]).astype(BF16)
        g = g_ref[...]

        def hidden(wg_ref, wu_ref, gate):
            return (jax.nn.silu(_dot(xb, wg_ref[0, 0].astype(BF16))) * _dot(xb, wu_ref[0, 0].astype(BF16))
                    * gate).astype(BF16)

        hid_a = hidden(wga_ref, wua_ref, g[:, 0:1])
        hid_b = hidden(wgb_ref, wub_ref, g[:, 1:2])
        for r in range(bm):
            row_out(nxt, r).start(priority=r % 2)
        wait_scatter(cur)
        y = x + _dot(hid_a, wda_ref[0, 0].astype(BF16)) + _dot(hid_b, wdb_ref[0, 0].astype(BF16))
        if final_norm:
            y = _rms(y, fw_ref[...])
        _store_rows(obufs[cur], y)

    for s in range(2):
        @pl.when((i < nused) & (slot == s))
        def _(s=s):
            run_block(s, 1 - s)

    @pl.when(i == nused)
    def _():
        for s in range(2):
            @pl.when(slot == s)
            def _(s=s):
                wait_gather(s)
                wait_scatter(s)

                def body(r, c):
                    row_out(1 - s, r).start()
                    return c
                lax.fori_loop(0, bm, body, 0, unroll=8)
                wait_scatter(1 - s)


def _moe_ffn(h, n_tok, nused, blk_ea, blk_eb, gather_rows, scatter_rows, gate_rows, nw, layer, wg, wu, wd, fw,
             final_norm):
    bm = MOE_BLOCK
    n_blocks = gather_rows.shape[0]
    rpo = 1 if final_norm else N_LANE_TILES
    out_cols = D_MODEL // rpo
    cur = lambda i, nu, ea, eb: (i, 0, 0)
    nxt = lambda i, nu, ea, eb: (jnp.minimum(i + 1, n_blocks - 1), 0, 0)
    idx_spec = lambda imap: pl.BlockSpec((1, 1, bm), imap, memory_space=pltpu.SMEM)
    fixed = lambda i, nu, ea, eb: (0, 0)
    wa = lambda i, nu, ea, eb: (layer, ea[i], 0, 0)
    wb = lambda i, nu, ea, eb: (layer, eb[i], 0, 0)
    up = pl.BlockSpec((1, 1, D_MODEL, EXPERT_FF), wa)
    upb = pl.BlockSpec((1, 1, D_MODEL, EXPERT_FF), wb)
    grid_spec = pltpu.PrefetchScalarGridSpec(
        num_scalar_prefetch=3,
        grid=(n_blocks,),
        in_specs=[idx_spec(cur), idx_spec(nxt), idx_spec(cur),
                  pl.BlockSpec((bm, 2), lambda i, nu, ea, eb: (i, 0)),
                  pl.BlockSpec((1, D_MODEL), fixed),
                  up, up, pl.BlockSpec((1, 1, EXPERT_FF, D_MODEL), wa),
                  upb, upb, pl.BlockSpec((1, 1, EXPERT_FF, D_MODEL), wb),
                  pl.BlockSpec((1, D_MODEL), fixed),
                  pl.BlockSpec(memory_space=pl.ANY)],
        out_specs=pl.BlockSpec(memory_space=pl.ANY),
        scratch_shapes=[pltpu.VMEM((bm * N_LANE_TILES, LANES), F32)] * 2 + [pltpu.VMEM((bm * rpo, out_cols), F32)] * 2
        + [pltpu.SemaphoreType.DMA((2,)), pltpu.SemaphoreType.DMA((2,))],
    )
    return pl.pallas_call(
        functools.partial(_ffn_kernel, bm=bm, n_tok=n_tok, rpo=rpo, final_norm=final_norm),
        grid_spec=grid_spec,
        out_shape=jax.ShapeDtypeStruct(((n_tok + 2 * bm) * rpo, out_cols), F32),
        compiler_params=_params("arbitrary"),
        name="moe_ffn",
    )(nused, blk_ea, blk_eb, gather_rows, gather_rows, scatter_rows, gate_rows, nw, wg, wu, wd, wg, wu, wd,
      fw, h)


_PAIR_A = np.array([a for a in range(EXPERTS_PER_GROUP) for b in range(a + 1, EXPERTS_PER_GROUP)], np.int32)
_PAIR_B = np.array([b for a in range(EXPERTS_PER_GROUP) for b in range(a + 1, EXPERTS_PER_GROUP)], np.int32)


def _moe_layer(h, n_tok, route_i, route_f, counts, nw, layer, wg, wu, wd, fw, final_norm):
    bm = MOE_BLOCK
    n_blocks = n_tok // bm + N_BUCKETS + 1
    bucket = route_i[0]
    rank = route_i[1]
    cnt = counts[:N_BUCKETS, 0].astype(jnp.int32)
    nblk = (cnt + bm - 1) // bm
    blk_end = jnp.cumsum(nblk)
    blk_start = blk_end - nblk
    in_bucket = bucket[:, None] == jnp.arange(N_BUCKETS, dtype=jnp.int32)[None, :]
    dest = jnp.sum(jnp.where(in_bucket, blk_start[None, :], 0), axis=1) * bm + rank
    upd = jnp.stack([jnp.arange(n_tok, dtype=jnp.int32), lax.bitcast_convert_type(route_f[0], jnp.int32),
                     lax.bitcast_convert_type(route_f[1], jnp.int32)], axis=1)
    init = jnp.broadcast_to(jnp.array([n_tok, 0, 0], jnp.int32), (n_blocks * bm, 3))
    packed = init.at[dest].set(upd, unique_indices=True)
    tok_rows = packed[:, 0]
    gate_rows = lax.bitcast_convert_type(packed[:, 1:3], F32)
    blk_bucket = jnp.minimum(
        jnp.searchsorted(blk_end, jnp.arange(n_blocks, dtype=jnp.int32), side='right'),
        N_BUCKETS - 1).astype(jnp.int32)
    nused = blk_end[-1:].astype(jnp.int32)
    last = blk_bucket[jnp.maximum(nused[0] - 1, 0)]
    blk_bucket = jnp.where(jnp.arange(n_blocks) < nused[0], blk_bucket, last)
    grp = blk_bucket // N_PAIRS
    pair = blk_bucket % N_PAIRS
    blk_ea = grp * EXPERTS_PER_GROUP + jnp.asarray(_PAIR_A)[pair]
    blk_eb = grp * EXPERTS_PER_GROUP + jnp.asarray(_PAIR_B)[pair]
    row = jnp.arange(n_blocks * bm, dtype=jnp.int32)
    gather_rows = jnp.minimum(tok_rows, n_tok - 1)
    scatter_rows = jnp.where(tok_rows >= n_tok, n_tok + (row // bm) % 2 * bm + row % bm, tok_rows)
    scatter_rows = jnp.concatenate([n_tok + bm + row[:bm], scatter_rows])
    return _moe_ffn(h, n_tok, nused, blk_ea, blk_eb, gather_rows.reshape(n_blocks, 1, bm),
                    scatter_rows.reshape(n_blocks + 1, 1, bm), gate_rows, nw, layer, wg, wu, wd, fw,
                    final_norm)


def _odd_in_kernel(h_ref, nw_ref, wz_ref, wx_ref, wdt_ref, wdtt_ref, cw_ref, cb_ref, dtb_ref, dtbt_ref,
                   z_ref, x_ref, b_ref, c_ref, dt_ref, dtt_ref, ubuf, *, tm, tiles_per_seq):
    i = pl.program_id(0)
    xb = _rms(_load_rows(h_ref), nw_ref[...]).astype(BF16)
    cw_all = cw_ref[...]
    cb_all = cb_ref[...]
    wchunk = 512

    @pl.when(i % tiles_per_seq == 0)
    def _():
        ubuf[0:SUBLANES, :] = jnp.zeros((SUBLANES, SSM_CONV_DIM), F32)

    for c0 in range(0, D_INNER, wchunk):
        z_ref[:, c0:c0 + wchunk] = _dot(xb, wz_ref[:, c0:c0 + wchunk]).astype(BF16)

    for c0 in range(0, SSM_CONV_DIM, wchunk):
        cols = slice(c0, c0 + wchunk)
        u = _dot(xb, wx_ref[:, cols])
        ubuf[SUBLANES:SUBLANES + tm, cols] = u
        cw = cw_all[:, cols]
        conv = cw[3:4, :] * u + cb_all[:, cols]
        for j in range(1, SSM_CONV_K):
            conv = conv + cw[3 - j:4 - j, :] * ubuf[SUBLANES - j:SUBLANES - j + tm, cols]
        act = jax.nn.silu(conv).astype(BF16)
        if c0 < D_INNER:
            x_ref[:, c0:c0 + wchunk] = act
        elif c0 < D_INNER + SSM_GN:
            b_ref[:, c0 - D_INNER:c0 - D_INNER + wchunk] = act
        else:
            c_ref[:, c0 - D_INNER - SSM_GN:c0 - D_INNER - SSM_GN + wchunk] = act
    ubuf[0:SUBLANES, :] = ubuf[tm:tm + SUBLANES, :]

    dt_ref[...] = jax.nn.softplus(_dot(xb, wdt_ref[...]) + dtb_ref[...])
    dtt_ref[...] = jax.nn.softplus(_dot_nt(wdtt_ref[...], xb) + dtbt_ref[...])


def _odd_in_proj(h, n_tok, seq, nw, wz, wx, wdt, wdtt, conv_w, conv_b, dtb, dtbt):
    tm = ROW_TILE
    row = lambda i: (i, 0)
    col = lambda i: (0, i)
    fixed = lambda i: (0, 0)
    full = lambda a: pl.BlockSpec(a.shape, fixed)
    return pl.pallas_call(
        functools.partial(_odd_in_kernel, tm=tm, tiles_per_seq=seq // tm),
        grid=(n_tok // tm,),
        in_specs=[_rows_spec(h, tm), full(nw), full(wz), full(wx), full(wdt), full(wdtt),
                  full(conv_w), full(conv_b), full(dtb), full(dtbt)],
        out_specs=[pl.BlockSpec((tm, D_INNER), row), pl.BlockSpec((tm, D_INNER), row),
                   pl.BlockSpec((tm, SSM_GN), row), pl.BlockSpec((tm, SSM_GN), row),
                   pl.BlockSpec((tm, SSM_HEADS), row), pl.BlockSpec((SSM_HEADS, tm), col)],
        out_shape=[jax.ShapeDtypeStruct((n_tok, D_INNER), BF16), jax.ShapeDtypeStruct((n_tok, D_INNER), BF16),
                   jax.ShapeDtypeStruct((n_tok, SSM_GN), BF16), jax.ShapeDtypeStruct((n_tok, SSM_GN), BF16),
                   jax.ShapeDtypeStruct((n_tok, SSM_HEADS), F32), jax.ShapeDtypeStruct((SSM_HEADS, n_tok), F32)],
        scratch_shapes=[pltpu.VMEM((tm + SUBLANES, SSM_CONV_DIM), F32)],
        compiler_params=_params("arbitrary"),
        name="odd_in_proj",
    )(h, nw, wz, wx, wdt, wdtt, conv_w, conv_b, dtb, dtbt)


def _ssd_kernel(x_ref, b_ref, c_ref, z_ref, dt_ref, dtt_ref, alog_ref, alogt_ref, dskip_ref, normw_ref,
                tril_ref, triu_ref, spread_ref, y_ref, state, *, lc):
    ci = pl.program_id(1)

    @pl.when(ci == 0)
    def _():
        state[...] = jnp.zeros(state.shape, F32)

    gw = SSM_HPG * SSM_HEAD_DIM
    hi = lax.Precision.HIGHEST
    dt = dt_ref[...]
    dtt = dtt_ref[...]
    a = -jnp.exp(alog_ref[...])
    at = -jnp.exp(alogt_ref[...])
    cs = jnp.dot(tril_ref[...], dt * a, precision=hi, preferred_element_type=F32)
    cst = jnp.dot(dtt * at, triu_ref[...], precision=hi, preferred_element_type=F32)
    cs_last = cs[lc - 1:lc, :]
    grow = jnp.exp(cs)
    dt_end = dt * jnp.exp(cs_last - cs)
    e_last = jnp.exp(cs_last)
    rr = lax.broadcasted_iota(jnp.int32, (lc, lc), 0)
    cc = lax.broadcasted_iota(jnp.int32, (lc, lc), 1)
    lower = cc <= rr
    lane = lax.broadcasted_iota(jnp.int32, (lc, gw), 1)
    head_of_lane = lane // SSM_HEAD_DIM
    fac = jnp.concatenate([dt, grow, dt_end, jnp.broadcast_to(e_last, (SUBLANES, SSM_HEADS))], axis=0)
    fac_hi = fac.astype(BF16).astype(F32)
    spread = _dot(jnp.concatenate([fac_hi, fac - fac_hi], axis=1).astype(BF16), spread_ref[...])

    for g in range(SSM_GROUPS):
        h0 = g * SSM_HPG
        bg = b_ref[:, g * SSM_STATE:(g + 1) * SSM_STATE]
        cg = c_ref[:, g * SSM_STATE:(g + 1) * SSM_STATE]
        xg = x_ref[:, g * gw:(g + 1) * gw].astype(F32)
        cb = _dot_nt(cg, bg)
        cols = slice(g * gw, (g + 1) * gw)
        xdt = xg * spread[0:lc, cols]
        xdt_b = xdt.astype(BF16)
        st = state[g]
        y = _dot(cg, st.astype(BF16)) * spread[lc:2 * lc, cols]
        for k in range(SSM_HPG):
            h = h0 + k
            seg = cs[:, h:h + 1] - cst[h:h + 1, :]
            decay = jnp.exp(jnp.where(lower, seg, -jnp.inf))
            m = (cb * decay).astype(BF16)
            y = y + _dot(m, jnp.where(head_of_lane == k, xdt_b, jnp.zeros_like(xdt_b)))
        x_end = (xg * spread[2 * lc:3 * lc, cols]).astype(BF16)
        bgt = bg.astype(F32).T.astype(BF16)
        state[g] = st * spread[3 * lc:3 * lc + 1, cols] + _dot(bgt, x_end)
        y = y + dskip_ref[:, g * gw:(g + 1) * gw] * xg
        y = y * jax.nn.silu(z_ref[:, g * gw:(g + 1) * gw].astype(F32))
        y = y * lax.rsqrt(jnp.mean(y * y, axis=-1, keepdims=True) + RMS_EPS)
        y_ref[:, g * gw:(g + 1) * gw] = (y * normw_ref[:, g * gw:(g + 1) * gw]).astype(BF16)


def _ssd(x, bm, cm, z, dt, dtt, n_tok, seq, alog, alogt, dskip, normw):
    lc = SSD_CHUNK
    nc = seq // lc
    tril = jnp.asarray(np.tril(np.ones((lc, lc), np.float32)))
    triu = jnp.asarray(np.triu(np.ones((lc, lc), np.float32)))
    head_of_col = np.arange(D_INNER) // SSM_HEAD_DIM
    spread = jnp.asarray(np.arange(2 * SSM_HEADS)[:, None] % SSM_HEADS == head_of_col[None, :], BF16)
    row = lambda b, c: (b * nc + c, 0)
    col = lambda b, c: (0, b * nc + c)
    fixed = lambda b, c: (0, 0)
    full = lambda a: pl.BlockSpec(a.shape, fixed)
    return pl.pallas_call(
        functools.partial(_ssd_kernel, lc=lc),
        grid=(n_tok // seq, nc),
        in_specs=[pl.BlockSpec((lc, D_INNER), row), pl.BlockSpec((lc, SSM_GN), row),
                  pl.BlockSpec((lc, SSM_GN), row), pl.BlockSpec((lc, D_INNER), row),
                  pl.BlockSpec((lc, SSM_HEADS), row), pl.BlockSpec((SSM_HEADS, lc), col),
                  full(alog), full(alogt), full(dskip), full(normw), full(tril), full(triu),
                  full(spread)],
        out_specs=pl.BlockSpec((lc, D_INNER), row),
        out_shape=jax.ShapeDtypeStruct((n_tok, D_INNER), BF16),
        scratch_shapes=[pltpu.VMEM((SSM_GROUPS, SSM_STATE, SSM_HPG * SSM_HEAD_DIM), F32)],
        compiler_params=_params("arbitrary", "arbitrary"),
        name="ssd_scan",
    )(x, bm, cm, z, dt, dtt, alog, alogt, dskip, normw, tril, triu, spread)


def _router_weights(wg, bg, we, be):
    w = jnp.concatenate([wg, jnp.moveaxis(we, 0, 1).reshape(D_MODEL, -1)], axis=1)
    b = jnp.concatenate([bg, be.reshape(-1)])
    pad = LANES - w.shape[1]
    w = jnp.pad(w, ((0, 0), (0, pad)))
    b = jnp.pad(b, (0, pad)).reshape(1, LANES)
    w_hi = w.astype(BF16)
    w_lo = (w - w_hi.astype(F32)).astype(BF16)
    return w_hi, w_lo, b


def kernel(x, even_w_in, even_conv_w, even_w_out, odd_in_proj, odd_conv_w, odd_conv_b, odd_dt_bias,
           odd_a_log, odd_d_skip, odd_norm_w, odd_out_proj, norm_mix, norm_ffn, router_g_w, router_g_b,
           router_e_w, router_e_b, expert_w_gate, expert_w_up, expert_w_down, final_norm):
    bsz, seq, d = x.shape
    n_tok = bsz * seq
    depth = norm_mix.shape[0]
    h = x.reshape(n_tok, d)
    fw = final_norm.reshape(1, d)
    for layer in range(depth):
        i = layer // 2
        nw = norm_mix[layer].reshape(1, d)
        if layer % 2 == 0:
            conv, q, k, v = _even_in_proj(h, n_tok, seq, nw, even_w_in[i].astype(BF16), even_conv_w[i])
            sb = _stick_breaking(q, k, v, n_tok, seq)
            w_out = even_w_out[i].astype(BF16)
            acts, ws = [conv, sb], [w_out[:CONV_WIDTH], w_out[CONV_WIDTH:]]
        else:
            w_in = odd_in_proj[i]
            wz = w_in[:, :D_INNER].astype(BF16)
            wx = w_in[:, D_INNER:D_INNER + SSM_CONV_DIM].astype(BF16)
            wdt = w_in[:, D_INNER + SSM_CONV_DIM:].astype(BF16)
            z, xs, bmat, cmat, dt, dtt = _odd_in_proj(
                h, n_tok, seq, nw, wz, wx, wdt, wdt.T, odd_conv_w[i], odd_conv_b[i].reshape(1, -1),
                odd_dt_bias[i].reshape(1, -1), odd_dt_bias[i].reshape(-1, 1))
            y = _ssd(xs, bmat, cmat, z, dt, dtt, n_tok, seq, odd_a_log[i].reshape(1, -1),
                     odd_a_log[i].reshape(-1, 1), jnp.repeat(odd_d_skip[i], SSM_HEAD_DIM).reshape(1, -1),
                     odd_norm_w[i].reshape(1, -1))
            acts, ws = [y], [odd_out_proj[i].astype(BF16)]
        wr_hi, wr_lo, br = _router_weights(router_g_w[layer], router_g_b[layer], router_e_w[layer],
                                           router_e_b[layer])
        h, route_i, route_f, counts = _mix_out_router(h, acts, ws, n_tok, norm_ffn[layer].reshape(1, d),
                                                      wr_hi, wr_lo, br)
        h = _moe_layer(h, n_tok, route_i, route_f, counts, norm_ffn[layer].reshape(1, d),
                       layer, expert_w_gate, expert_w_up, expert_w_down, fw, final_norm=(layer == depth - 1))
    return h[:n_tok].reshape(bsz, seq, d)
```

```python
import functools

import numpy as np
import jax
import jax.numpy as jnp
from jax import lax
from jax.experimental import pallas as pl
from jax.experimental.pallas import tpu as pltpu

F32 = jnp.float32
BF16 = jnp.bfloat16

D_MODEL = 1024
RMS_EPS = 1e-6
CONV_WIDTH = 512
SHORT_CONV_K = 3
SB_HEAD_DIM = 64
SB_WIDTH = 512
D_INNER = 2048
SSM_HEAD_DIM = 64
SSM_HEADS = 32
SSM_GROUPS = 8
SSM_HPG = 4
SSM_STATE = 128
SSM_CONV_K = 4
SSM_GN = SSM_GROUPS * SSM_STATE
SSM_CONV_DIM = D_INNER + 2 * SSM_GN
N_EXPERT_GROUPS = 4
EXPERTS_PER_GROUP = 8
EXPERT_FF = 512
N_PAIRS = EXPERTS_PER_GROUP * (EXPERTS_PER_GROUP - 1) // 2
N_BUCKETS = N_EXPERT_GROUPS * N_PAIRS

LANES = 128
SUBLANES = 8
N_LANE_TILES = D_MODEL // LANES
VMEM_LIMIT = 56 * 1024 * 1024

ROW_TILE = 512
SB_BLOCK = 256
SB_GROUP = 8
SSD_CHUNK = 128
MOE_BLOCK = 320


def _params(*sem):
    return pltpu.CompilerParams(dimension_semantics=sem, vmem_limit_bytes=VMEM_LIMIT)


def _rms(x, w):
    return x * lax.rsqrt(jnp.mean(x * x, axis=-1, keepdims=True) + RMS_EPS) * w


def _dot(a, b):
    return jnp.dot(a, b, preferred_element_type=F32)


def _load_rows(ref):
    if ref.shape[1] == D_MODEL:
        return ref[...]
    n = ref.shape[0] // N_LANE_TILES
    return jnp.concatenate([ref[pl.ds(s, n, stride=N_LANE_TILES), :] for s in range(N_LANE_TILES)], axis=1)


def _store_rows(ref, val):
    if ref.shape[1] == D_MODEL:
        ref[...] = val
    else:
        n = ref.shape[0] // N_LANE_TILES
        for s in range(N_LANE_TILES):
            ref[pl.ds(s, n, stride=N_LANE_TILES), :] = val[:, s * LANES:(s + 1) * LANES]


def _rows_spec(h, tm):
    if h.shape[1] == D_MODEL:
        return pl.BlockSpec((tm, D_MODEL), lambda i: (i, 0))
    return pl.BlockSpec((tm * N_LANE_TILES, LANES), lambda i: (i, 0))


def _dot_nt(a, b):
    return lax.dot_general(a, b, (((1,), (1,)), ((), ())), preferred_element_type=F32)


def _even_in_kernel(h_ref, nw_ref, w_ref, cw_ref, conv_ref, q_ref, k_ref, v_ref, ubuf,
                    *, tm, tiles_per_seq):
    i = pl.program_id(0)
    xb = _rms(_load_rows(h_ref), nw_ref[...]).astype(BF16)
    c = CONV_WIDTH

    def proj(c0):
        return _dot(xb, w_ref[:, c0:c0 + c])

    gate_b = proj(0)
    u = proj(c) * proj(2 * c)

    @pl.when(i % tiles_per_seq == 0)
    def _():
        ubuf[0:SUBLANES, :] = jnp.zeros((SUBLANES, c), F32)

    ubuf[SUBLANES:SUBLANES + tm, :] = u
    cw = cw_ref[...]
    conv = cw[2:3, :] * u + cw[1:2, :] * ubuf[SUBLANES - 1:SUBLANES - 1 + tm, :] \
        + cw[0:1, :] * ubuf[SUBLANES - 2:SUBLANES - 2 + tm, :]
    conv_ref[...] = (gate_b * conv).astype(BF16)
    ubuf[0:SUBLANES, :] = ubuf[tm:tm + SUBLANES, :]
    q_ref[...] = (proj(3 * c) * (SB_HEAD_DIM ** -0.5)).astype(BF16)
    k_ref[...] = proj(4 * c).astype(BF16)
    v_ref[...] = proj(5 * c).astype(BF16)


def _even_in_proj(h, n_tok, seq, nw, w_in, conv_w):
    tm = ROW_TILE
    c = CONV_WIDTH
    row = lambda i: (i, 0)
    fixed = lambda i: (0, 0)
    out = jax.ShapeDtypeStruct((n_tok, c), BF16)
    return pl.pallas_call(
        functools.partial(_even_in_kernel, tm=tm, tiles_per_seq=seq // tm),
        grid=(n_tok // tm,),
        in_specs=[_rows_spec(h, tm), pl.BlockSpec((1, D_MODEL), fixed),
                  pl.BlockSpec((D_MODEL, 6 * c), fixed), pl.BlockSpec((SHORT_CONV_K, c), fixed)],
        out_specs=[pl.BlockSpec((tm, c), row)] * 4,
        out_shape=[out] * 4,
        scratch_shapes=[pltpu.VMEM((tm + SUBLANES, c), F32)],
        compiler_params=_params("arbitrary"),
        name="even_in_proj",
    )(h, nw, w_in, conv_w)


def _sb_kernel(q_ref, k_ref, v_ref, tri_ref, o_ref, *, seq, bq, group):
    nq = seq // bq
    lane = lax.broadcasted_iota(jnp.int32, (bq, LANES), 1)
    rr = lax.broadcasted_iota(jnp.int32, (bq, bq), 0)
    cc = lax.broadcasted_iota(jnp.int32, (bq, bq), 1)
    causal = cc < rr
    every = cc >= 0
    tri = tri_ref[...]

    def block(qs, kb, vb, run, acc, mask):
        z = _dot_nt(qs, kb)
        sp = jnp.maximum(z, 0.0) + jnp.log(1.0 + jnp.exp(-jnp.abs(z)))
        spm = sp if mask is None else jnp.where(mask, sp, 0.0)
        suffix = _dot(spm.astype(BF16), tri) + run
        p = jnp.exp((z - sp) + suffix)
        if mask is not None:
            p = jnp.where(mask, p, 0.0)
        acc = acc + _dot(p.astype(BF16), vb)
        run = run - jnp.sum(spm, axis=1, keepdims=True)
        return run, acc

    def kv(kj):
        rows = pl.ds(pl.multiple_of(kj * bq, bq), bq)
        return k_ref[rows, :], v_ref[rows, :]

    def heads(qi):
        q = q_ref[qi * bq:(qi + 1) * bq, :]
        return [jnp.where((lane < SB_HEAD_DIM) == (head == 0), q, jnp.zeros_like(q)) for head in range(2)]

    def store(qi, acc):
        o_ref[qi * bq:(qi + 1) * bq, :] = jnp.where(lane < SB_HEAD_DIM, acc[:bq], acc[bq:]).astype(BF16)

    for q0 in range(0, nq, group):
        qhs = [heads(q0 + j) for j in range(group)]
        run = acc = None
        for kk in range(group - 1, -1, -1):
            kb, vb = kv(q0 + kk)
            qs = jnp.concatenate([q for j in range(kk, group) for q in qhs[j]], axis=0)
            mask = jnp.concatenate([causal, causal] + [every] * (2 * (group - 1 - kk)), axis=0)
            run0 = jnp.zeros((2 * bq, 1), F32)
            acc0 = jnp.zeros((2 * bq, LANES), F32)
            run = run0 if run is None else jnp.concatenate([run0, run], axis=0)
            acc = acc0 if acc is None else jnp.concatenate([acc0, acc], axis=0)
            run, acc = block(qs, kb, vb, run, acc, mask)
        if q0 > 0:
            def body(it, cr, qs=qs, q0=q0):
                kb, vb = kv(q0 - 1 - it)
                return block(qs, kb, vb, cr[0], cr[1], None)
            run, acc = lax.fori_loop(0, q0, body, (run, acc))
        for j in range(group):
            store(q0 + j, acc[2 * j * bq:2 * (j + 1) * bq])


def _stick_breaking(q, k, v, n_tok, seq):
    bq = SB_BLOCK
    tri = jnp.asarray(-np.tril(np.ones((bq, bq), np.float32), -1), BF16)
    blk = pl.BlockSpec((seq, LANES), lambda b, hp: (b, hp))
    return pl.pallas_call(
        functools.partial(_sb_kernel, seq=seq, bq=bq, group=SB_GROUP),
        grid=(n_tok // seq, SB_WIDTH // LANES),
        in_specs=[blk, blk, blk, pl.BlockSpec((bq, bq), lambda b, hp: (0, 0))],
        out_specs=blk,
        out_shape=jax.ShapeDtypeStruct((n_tok, SB_WIDTH), BF16),
        compiler_params=_params("arbitrary", "arbitrary"),
        name="stick_breaking",
    )(q, k, v, tri)


def _mix_out_kernel(*refs, n_act, tm):
    h_ref = refs[0]
    act_refs = refs[1:1 + n_act]
    w_refs = refs[1 + n_act:1 + 2 * n_act]
    nw_ref, wrh_ref, wrl_ref, br_ref, triu_ref = refs[1 + 2 * n_act:6 + 2 * n_act]
    hout_ref, ri_ref, rf_ref, cnt_ref, carry = refs[6 + 2 * n_act:]
    i = pl.program_id(0)

    @pl.when(i == 0)
    def _():
        carry[...] = jnp.zeros(carry.shape, F32)

    acc = _load_rows(h_ref)
    for a_ref, w_ref in zip(act_refs, w_refs):
        acc = acc + _dot(a_ref[...], w_ref[...])
    _store_rows(hout_ref, acc)

    xn = _rms(acc, nw_ref[...])
    xh = xn.astype(BF16)
    xl = (xn - xh.astype(F32)).astype(BF16)
    wrh = wrh_ref[...]
    logits = _dot(xh, wrh) + _dot(xl, wrh) + _dot(xh, wrl_ref[...]) + br_ref[...]
    lt = logits.T
    r = lax.broadcasted_iota(jnp.int32, (LANES, tm), 0).astype(F32)
    big = float(LANES)
    neg = -jnp.inf

    def first_argmax(vals):
        m = jnp.max(vals, axis=0, keepdims=True)
        return m, jnp.min(jnp.where(vals == m, r, big), axis=0, keepdims=True)

    gl = jnp.where(r < N_EXPERT_GROUPS, lt, neg)
    gmax, gidx = first_argmax(gl)
    g_top_p = 1.0 / jnp.sum(jnp.exp(gl - gmax), axis=0, keepdims=True)
    lo = N_EXPERT_GROUPS + EXPERTS_PER_GROUP * gidx
    el = jnp.where((r >= lo) & (r < lo + EXPERTS_PER_GROUP), lt, neg)
    m1, i1 = first_argmax(el)
    m2, i2 = first_argmax(jnp.where(r == i1, neg, el))
    t = jnp.exp(m2 - m1)
    g1 = g_top_p / (1.0 + t)
    g2 = g_top_p * t / (1.0 + t)
    e1 = i1 - lo
    e2 = i2 - lo
    ea = jnp.minimum(e1, e2)
    eb = jnp.maximum(e1, e2)
    pair = ea * (2 * EXPERTS_PER_GROUP - 1 - ea) * 0.5 + (eb - ea - 1.0)
    bucket = gidx * N_PAIRS + pair
    first_is_a = e1 < e2
    gate_a = jnp.where(first_is_a, g1, g2)
    gate_b = jnp.where(first_is_a, g2, g1)

    onehot = r == bucket
    cum = _dot(jnp.where(onehot, 1.0, 0.0).astype(BF16), triu_ref[...])
    prev = carry[...]
    rank = jnp.sum(jnp.where(onehot, cum + prev[:, 0:1], 0.0), axis=0, keepdims=True)
    new = prev + jnp.sum(jnp.where(onehot, 1.0, 0.0), axis=1, keepdims=True)
    carry[...] = new
    cnt_ref[...] = new

    row8 = lax.broadcasted_iota(jnp.int32, (SUBLANES, tm), 0)
    ri_ref[...] = jnp.where(row8 == 0, bucket, jnp.where(row8 == 1, rank, 0.0)).astype(jnp.int32)
    rf_ref[...] = jnp.where(row8 == 0, gate_a, jnp.where(row8 == 1, gate_b, 0.0))


def _mix_out_router(h, acts, ws, n_tok, nw, wr_hi, wr_lo, br):
    tm = ROW_TILE
    row = lambda i: (i, 0)
    col = lambda i: (0, i)
    fixed = lambda i: (0, 0)
    triu = jnp.asarray(np.triu(np.ones((tm, tm), np.float32), 1), BF16)
    in_specs = [_rows_spec(h, tm)]
    in_specs += [pl.BlockSpec((tm, a.shape[1]), row) for a in acts]
    in_specs += [pl.BlockSpec(w.shape, fixed) for w in ws]
    in_specs += [pl.BlockSpec((1, D_MODEL), fixed), pl.BlockSpec((D_MODEL, LANES), fixed),
                 pl.BlockSpec((D_MODEL, LANES), fixed), pl.BlockSpec((1, LANES), fixed),
                 pl.BlockSpec((tm, tm), fixed)]
    return pl.pallas_call(
        functools.partial(_mix_out_kernel, n_act=len(acts), tm=tm),
        grid=(n_tok // tm,),
        in_specs=in_specs,
        out_specs=[pl.BlockSpec((tm * N_LANE_TILES, LANES), row), pl.BlockSpec((SUBLANES, tm), col),
                   pl.BlockSpec((SUBLANES, tm), col), pl.BlockSpec((LANES, LANES), fixed)],
        out_shape=[jax.ShapeDtypeStruct((n_tok * N_LANE_TILES, LANES), F32),
                   jax.ShapeDtypeStruct((SUBLANES, n_tok), jnp.int32),
                   jax.ShapeDtypeStruct((SUBLANES, n_tok), F32),
                   jax.ShapeDtypeStruct((LANES, LANES), F32)],
        scratch_shapes=[pltpu.VMEM((LANES, LANES), F32)],
        compiler_params=_params("arbitrary"),
        name="mix_out_router",
    )(h, *acts, *ws, nw, wr_hi, wr_lo, br, triu)


def _ffn_kernel(nused_ref, ea_ref, eb_ref, gcur_ref, gnext_ref, sprev_ref, g_ref, nw_ref,
                wga_ref, wua_ref, wda_ref, wgb_ref, wub_ref, wdb_ref, fw_ref, h_hbm,
                out_hbm, xbuf0, xbuf1, obuf0, obuf1, gsem, ssem, *, bm, n_tok, rpo, final_norm):
    del ea_ref, eb_ref
    i = pl.program_id(0)
    slot = i % 2
    nused = nused_ref[0]
    xbufs = (xbuf0, xbuf1)
    obufs = (obuf0, obuf1)

    rpi = N_LANE_TILES

    def tok_rows(t, n):
        return pl.ds(t * n if isinstance(t, int) else pl.multiple_of(t * n, n), n)

    def row_in(gref, s, r):
        return pltpu.make_async_copy(h_hbm.at[tok_rows(gref[0, 0, r], rpi)], xbufs[s].at[tok_rows(r, rpi)],
                                     gsem.at[s])

    def row_out(s, r):
        return pltpu.make_async_copy(obufs[s].at[tok_rows(r, rpo)], out_hbm.at[tok_rows(sprev_ref[0, 0, r], rpo)],
                                     ssem.at[s])

    def wait_gather(s):
        pltpu.make_async_copy(h_hbm.at[pl.ds(0, bm * rpi)], xbufs[s], gsem.at[s]).wait()

    def wait_scatter(s):
        pltpu.make_async_copy(obufs[s], out_hbm.at[pl.ds(0, bm * rpo)], ssem.at[s]).wait()

    @pl.when(i == 0)
    def _():
        def body(r, c):
            row_in(gcur_ref, 0, r).start()
            return c
        lax.fori_loop(0, bm, body, 0, unroll=8)
        for s in range(2):
            obufs[s][...] = jnp.zeros(obufs[s].shape, F32)
            cp = pltpu.make_async_copy(obufs[s], out_hbm.at[pl.ds((n_tok + s * bm) * rpo, bm * rpo)], ssem.at[s])
            cp.start()
            if s == 1:
                cp.wait()

    def run_block(cur, nxt):
        for r in range(bm):
            row_in(gnext_ref, nxt, r).start(priority=r % 2)
        wait_gather(cur)
        x = _load_rows(xbufs[cur])
        xb = _rms(x, nw_ref[...]).astype(BF16)
        g = g_ref[...]

        def hidden(wg_ref, wu_ref, gate):
            return (jax.nn.silu(_dot(xb, wg_ref[0, 0].astype(BF16))) * _dot(xb, wu_ref[0, 0].astype(BF16))
                    * gate).astype(BF16)

        hid_a = hidden(wga_ref, wua_ref, g[:, 0:1])
        hid_b = hidden(wgb_ref, wub_ref, g[:, 1:2])
        for r in range(bm):
            row_out(nxt, r).start(priority=r % 2)
        wait_scatter(cur)
        y = x + _dot(hid_a, wda_ref[0, 0].astype(BF16)) + _dot(hid_b, wdb_ref[0, 0].astype(BF16))
        if final_norm:
            y = _rms(y, fw_ref[...])
        _store_rows(obufs[cur], y)

    for s in range(2):
        @pl.when((i < nused) & (slot == s))
        def _(s=s):
            run_block(s, 1 - s)

    @pl.when(i == nused)
    def _():
        for s in range(2):
            @pl.when(slot == s)
            def _(s=s):
                wait_gather(s)
                wait_scatter(s)

                def body(r, c):
                    row_out(1 - s, r).start()
                    return c
                lax.fori_loop(0, bm, body, 0, unroll=8)
                wait_scatter(1 - s)


def _moe_ffn(h, n_tok, nused, blk_ea, blk_eb, gather_rows, scatter_rows, gate_rows, nw, layer, wg, wu, wd, fw,
             final_norm):
    bm = MOE_BLOCK
    n_blocks = gather_rows.shape[0]
    rpo = N_LANE_TILES
    out_cols = D_MODEL // rpo
    cur = lambda i, nu, ea, eb: (i, 0, 0)
    nxt = lambda i, nu, ea, eb: (jnp.minimum(i + 1, n_blocks - 1), 0, 0)
    idx_spec = lambda imap: pl.BlockSpec((1, 1, bm), imap, memory_space=pltpu.SMEM)
    fixed = lambda i, nu, ea, eb: (0, 0)
    wa = lambda i, nu, ea, eb: (layer, ea[i], 0, 0)
    wb = lambda i, nu, ea, eb: (layer, eb[i], 0, 0)
    up = pl.BlockSpec((1, 1, D_MODEL, EXPERT_FF), wa)
    upb = pl.BlockSpec((1, 1, D_MODEL, EXPERT_FF), wb)
    grid_spec = pltpu.PrefetchScalarGridSpec(
        num_scalar_prefetch=3,
        grid=(n_blocks,),
        in_specs=[idx_spec(cur), idx_spec(nxt), idx_spec(cur),
                  pl.BlockSpec((bm, 2), lambda i, nu, ea, eb: (i, 0)),
                  pl.BlockSpec((1, D_MODEL), fixed),
                  up, up, pl.BlockSpec((1, 1, EXPERT_FF, D_MODEL), wa),
                  upb, upb, pl.BlockSpec((1, 1, EXPERT_FF, D_MODEL), wb),
                  pl.BlockSpec((1, D_MODEL), fixed),
                  pl.BlockSpec(memory_space=pl.ANY)],
        out_specs=pl.BlockSpec(memory_space=pl.ANY),
        scratch_shapes=[pltpu.VMEM((bm * N_LANE_TILES, LANES), F32)] * 2 + [pltpu.VMEM((bm * rpo, out_cols), F32)] * 2
        + [pltpu.SemaphoreType.DMA((2,)), pltpu.SemaphoreType.DMA((2,))],
    )
    return pl.pallas_call(
        functools.partial(_ffn_kernel, bm=bm, n_tok=n_tok, rpo=rpo, final_norm=final_norm),
        grid_spec=grid_spec,
        out_shape=jax.ShapeDtypeStruct(((n_tok + 2 * bm) * rpo, out_cols), F32),
        compiler_params=_params("arbitrary"),
        name="moe_ffn",
    )(nused, blk_ea, blk_eb, gather_rows, gather_rows, scatter_rows, gate_rows, nw, wg, wu, wd, wg, wu, wd,
      fw, h)


_PAIR_A = np.array([a for a in range(EXPERTS_PER_GROUP) for b in range(a + 1, EXPERTS_PER_GROUP)], np.int32)
_PAIR_B = np.array([b for a in range(EXPERTS_PER_GROUP) for b in range(a + 1, EXPERTS_PER_GROUP)], np.int32)


def _moe_layer(h, n_tok, route_i, route_f, counts, nw, layer, wg, wu, wd, fw, final_norm):
    bm = MOE_BLOCK
    n_blocks = n_tok // bm + N_BUCKETS + 1
    bucket = route_i[0]
    rank = route_i[1]
    cnt = counts[:N_BUCKETS, 0].astype(jnp.int32)
    nblk = (cnt + bm - 1) // bm
    blk_end = jnp.cumsum(nblk)
    blk_start = blk_end - nblk
    in_bucket = bucket[:, None] == jnp.arange(N_BUCKETS, dtype=jnp.int32)[None, :]
    dest = jnp.sum(jnp.where(in_bucket, blk_start[None, :], 0), axis=1) * bm + rank
    upd = jnp.stack([jnp.arange(n_tok, dtype=jnp.int32), lax.bitcast_convert_type(route_f[0], jnp.int32),
                     lax.bitcast_convert_type(route_f[1], jnp.int32)], axis=1)
    init = jnp.broadcast_to(jnp.array([n_tok, 0, 0], jnp.int32), (n_blocks * bm, 3))
    packed = init.at[dest].set(upd, unique_indices=True)
    tok_rows = packed[:, 0]
    gate_rows = lax.bitcast_convert_type(packed[:, 1:3], F32)
    blk_bucket = jnp.minimum(
        jnp.searchsorted(blk_end, jnp.arange(n_blocks, dtype=jnp.int32), side='right'),
        N_BUCKETS - 1).astype(jnp.int32)
    nused = blk_end[-1:].astype(jnp.int32)
    last = blk_bucket[jnp.maximum(nused[0] - 1, 0)]
    blk_bucket = jnp.where(jnp.arange(n_blocks) < nused[0], blk_bucket, last)
    grp = blk_bucket // N_PAIRS
    pair = blk_bucket % N_PAIRS
    blk_ea = grp * EXPERTS_PER_GROUP + jnp.asarray(_PAIR_A)[pair]
    blk_eb = grp * EXPERTS_PER_GROUP + jnp.asarray(_PAIR_B)[pair]
    row = jnp.arange(n_blocks * bm, dtype=jnp.int32)
    gather_rows = jnp.minimum(tok_rows, n_tok - 1)
    scatter_rows = jnp.where(tok_rows >= n_tok, n_tok + (row // bm) % 2 * bm + row % bm, tok_rows)
    scatter_rows = jnp.concatenate([n_tok + bm + row[:bm], scatter_rows])
    return _moe_ffn(h, n_tok, nused, blk_ea, blk_eb, gather_rows.reshape(n_blocks, 1, bm),
                    scatter_rows.reshape(n_blocks + 1, 1, bm), gate_rows, nw, layer, wg, wu, wd, fw,
                    final_norm)


def _odd_in_kernel(h_ref, nw_ref, wz_ref, wx_ref, wdt_ref, wdtt_ref, cw_ref, cb_ref, dtb_ref, dtbt_ref,
                   z_ref, x_ref, b_ref, c_ref, dt_ref, dtt_ref, ubuf, *, tm, tiles_per_seq):
    i = pl.program_id(0)
    xb = _rms(_load_rows(h_ref), nw_ref[...]).astype(BF16)
    cw_all = cw_ref[...]
    cb_all = cb_ref[...]
    wchunk = 512

    @pl.when(i % tiles_per_seq == 0)
    def _():
        ubuf[0:SUBLANES, :] = jnp.zeros((SUBLANES, SSM_CONV_DIM), F32)

    for c0 in range(0, D_INNER, wchunk):
        z_ref[:, c0:c0 + wchunk] = _dot(xb, wz_ref[:, c0:c0 + wchunk]).astype(BF16)

    for c0 in range(0, SSM_CONV_DIM, wchunk):
        cols = slice(c0, c0 + wchunk)
        u = _dot(xb, wx_ref[:, cols])
        ubuf[SUBLANES:SUBLANES + tm, cols] = u
        cw = cw_all[:, cols]
        conv = cw[3:4, :] * u + cb_all[:, cols]
        for j in range(1, SSM_CONV_K):
            conv = conv + cw[3 - j:4 - j, :] * ubuf[SUBLANES - j:SUBLANES - j + tm, cols]
        act = jax.nn.silu(conv).astype(BF16)
        if c0 < D_INNER:
            x_ref[:, c0:c0 + wchunk] = act
        elif c0 < D_INNER + SSM_GN:
            b_ref[:, c0 - D_INNER:c0 - D_INNER + wchunk] = act
        else:
            c_ref[:, c0 - D_INNER - SSM_GN:c0 - D_INNER - SSM_GN + wchunk] = act
    ubuf[0:SUBLANES, :] = ubuf[tm:tm + SUBLANES, :]

    dt_ref[...] = jax.nn.softplus(_dot(xb, wdt_ref[...]) + dtb_ref[...])
    dtt_ref[...] = jax.nn.softplus(_dot_nt(wdtt_ref[...], xb) + dtbt_ref[...])


def _odd_in_proj(h, n_tok, seq, nw, wz, wx, wdt, wdtt, conv_w, conv_b, dtb, dtbt):
    tm = ROW_TILE
    row = lambda i: (i, 0)
    col = lambda i: (0, i)
    fixed = lambda i: (0, 0)
    full = lambda a: pl.BlockSpec(a.shape, fixed)
    return pl.pallas_call(
        functools.partial(_odd_in_kernel, tm=tm, tiles_per_seq=seq // tm),
        grid=(n_tok // tm,),
        in_specs=[_rows_spec(h, tm), full(nw), full(wz), full(wx), full(wdt), full(wdtt),
                  full(conv_w), full(conv_b), full(dtb), full(dtbt)],
        out_specs=[pl.BlockSpec((tm, D_INNER), row), pl.BlockSpec((tm, D_INNER), row),
                   pl.BlockSpec((tm, SSM_GN), row), pl.BlockSpec((tm, SSM_GN), row),
                   pl.BlockSpec((tm, SSM_HEADS), row), pl.BlockSpec((SSM_HEADS, tm), col)],
        out_shape=[jax.ShapeDtypeStruct((n_tok, D_INNER), BF16), jax.ShapeDtypeStruct((n_tok, D_INNER), BF16),
                   jax.ShapeDtypeStruct((n_tok, SSM_GN), BF16), jax.ShapeDtypeStruct((n_tok, SSM_GN), BF16),
                   jax.ShapeDtypeStruct((n_tok, SSM_HEADS), F32), jax.ShapeDtypeStruct((SSM_HEADS, n_tok), F32)],
        scratch_shapes=[pltpu.VMEM((tm + SUBLANES, SSM_CONV_DIM), F32)],
        compiler_params=_params("arbitrary"),
        name="odd_in_proj",
    )(h, nw, wz, wx, wdt, wdtt, conv_w, conv_b, dtb, dtbt)


def _ssd_kernel(x_ref, b_ref, c_ref, z_ref, dt_ref, dtt_ref, alog_ref, alogt_ref, dskip_ref, normw_ref,
                tril_ref, triu_ref, spread_ref, y_ref, state, *, lc):
    ci = pl.program_id(1)

    @pl.when(ci == 0)
    def _():
        state[...] = jnp.zeros(state.shape, F32)

    gw = SSM_HPG * SSM_HEAD_DIM
    hi = lax.Precision.HIGHEST
    dt = dt_ref[...]
    dtt = dtt_ref[...]
    a = -jnp.exp(alog_ref[...])
    at = -jnp.exp(alogt_ref[...])
    cs = jnp.dot(tril_ref[...], dt * a, precision=hi, preferred_element_type=F32)
    cst = jnp.dot(dtt * at, triu_ref[...], precision=hi, preferred_element_type=F32)
    cs_last = cs[lc - 1:lc, :]
    grow = jnp.exp(cs)
    dt_end = dt * jnp.exp(cs_last - cs)
    e_last = jnp.exp(cs_last)
    rr = lax.broadcasted_iota(jnp.int32, (lc, lc), 0)
    cc = lax.broadcasted_iota(jnp.int32, (lc, lc), 1)
    lower = cc <= rr
    lane = lax.broadcasted_iota(jnp.int32, (lc, gw), 1)
    head_of_lane = lane // SSM_HEAD_DIM
    fac = jnp.concatenate([dt, grow, dt_end, jnp.broadcast_to(e_last, (SUBLANES, SSM_HEADS))], axis=0)
    fac_hi = fac.astype(BF16).astype(F32)
    spread = _dot(jnp.concatenate([fac_hi, fac - fac_hi], axis=1).astype(BF16), spread_ref[...])

    for g in range(SSM_GROUPS):
        h0 = g * SSM_HPG
        bg = b_ref[:, g * SSM_STATE:(g + 1) * SSM_STATE]
        cg = c_ref[:, g * SSM_STATE:(g + 1) * SSM_STATE]
        xg = x_ref[:, g * gw:(g + 1) * gw].astype(F32)
        cb = _dot_nt(cg, bg)
        cols = slice(g * gw, (g + 1) * gw)
        xdt = xg * spread[0:lc, cols]
        xdt_b = xdt.astype(BF16)
        st = state[g]
        y = _dot(cg, st.astype(BF16)) * spread[lc:2 * lc, cols]
        for k in range(SSM_HPG):
            h = h0 + k
            seg = cs[:, h:h + 1] - cst[h:h + 1, :]
            decay = jnp.exp(jnp.where(lower, seg, -jnp.inf))
            m = (cb * decay).astype(BF16)
            y = y + _dot(m, jnp.where(head_of_lane == k, xdt_b, jnp.zeros_like(xdt_b)))
        x_end = (xg * spread[2 * lc:3 * lc, cols]).astype(BF16)
        bgt = bg.astype(F32).T.astype(BF16)
        state[g] = st * spread[3 * lc:3 * lc + 1, cols] + _dot(bgt, x_end)
        y = y + dskip_ref[:, g * gw:(g + 1) * gw] * xg
        y = y * jax.nn.silu(z_ref[:, g * gw:(g + 1) * gw].astype(F32))
        y = y * lax.rsqrt(jnp.mean(y * y, axis=-1, keepdims=True) + RMS_EPS)
        y_ref[:, g * gw:(g + 1) * gw] = (y * normw_ref[:, g * gw:(g + 1) * gw]).astype(BF16)


def _ssd(x, bm, cm, z, dt, dtt, n_tok, seq, alog, alogt, dskip, normw):
    lc = SSD_CHUNK
    nc = seq // lc
    tril = jnp.asarray(np.tril(np.ones((lc, lc), np.float32)))
    triu = jnp.asarray(np.triu(np.ones((lc, lc), np.float32)))
    head_of_col = np.arange(D_INNER) // SSM_HEAD_DIM
    spread = jnp.asarray(np.arange(2 * SSM_HEADS)[:, None] % SSM_HEADS == head_of_col[None, :], BF16)
    row = lambda b, c: (b * nc + c, 0)
    col = lambda b, c: (0, b * nc + c)
    fixed = lambda b, c: (0, 0)
    full = lambda a: pl.BlockSpec(a.shape, fixed)
    return pl.pallas_call(
        functools.partial(_ssd_kernel, lc=lc),
        grid=(n_tok // seq, nc),
        in_specs=[pl.BlockSpec((lc, D_INNER), row), pl.BlockSpec((lc, SSM_GN), row),
                  pl.BlockSpec((lc, SSM_GN), row), pl.BlockSpec((lc, D_INNER), row),
                  pl.BlockSpec((lc, SSM_HEADS), row), pl.BlockSpec((SSM_HEADS, lc), col),
                  full(alog), full(alogt), full(dskip), full(normw), full(tril), full(triu),
                  full(spread)],
        out_specs=pl.BlockSpec((lc, D_INNER), row),
        out_shape=jax.ShapeDtypeStruct((n_tok, D_INNER), BF16),
        scratch_shapes=[pltpu.VMEM((SSM_GROUPS, SSM_STATE, SSM_HPG * SSM_HEAD_DIM), F32)],
        compiler_params=_params("arbitrary", "arbitrary"),
        name="ssd_scan",
    )(x, bm, cm, z, dt, dtt, alog, alogt, dskip, normw, tril, triu, spread)


def _router_weights(wg, bg, we, be):
    w = jnp.concatenate([wg, jnp.moveaxis(we, 0, 1).reshape(D_MODEL, -1)], axis=1)
    b = jnp.concatenate([bg, be.reshape(-1)])
    pad = LANES - w.shape[1]
    w = jnp.pad(w, ((0, 0), (0, pad)))
    b = jnp.pad(b, (0, pad)).reshape(1, LANES)
    w_hi = w.astype(BF16)
    w_lo = (w - w_hi.astype(F32)).astype(BF16)
    return w_hi, w_lo, b


def kernel(x, even_w_in, even_conv_w, even_w_out, odd_in_proj, odd_conv_w, odd_conv_b, odd_dt_bias,
           odd_a_log, odd_d_skip, odd_norm_w, odd_out_proj, norm_mix, norm_ffn, router_g_w, router_g_b,
           router_e_w, router_e_b, expert_w_gate, expert_w_up, expert_w_down, final_norm):
    bsz, seq, d = x.shape
    n_tok = bsz * seq
    depth = norm_mix.shape[0]
    h = x.reshape(n_tok, d)
    fw = final_norm.reshape(1, d)
    for layer in range(depth):
        i = layer // 2
        nw = norm_mix[layer].reshape(1, d)
        if layer % 2 == 0:
            conv, q, k, v = _even_in_proj(h, n_tok, seq, nw, even_w_in[i].astype(BF16), even_conv_w[i])
            sb = _stick_breaking(q, k, v, n_tok, seq)
            w_out = even_w_out[i].astype(BF16)
            acts, ws = [conv, sb], [w_out[:CONV_WIDTH], w_out[CONV_WIDTH:]]
        else:
            w_in = odd_in_proj[i]
            wz = w_in[:, :D_INNER].astype(BF16)
            wx = w_in[:, D_INNER:D_INNER + SSM_CONV_DIM].astype(BF16)
            wdt = w_in[:, D_INNER + SSM_CONV_DIM:].astype(BF16)
            z, xs, bmat, cmat, dt, dtt = _odd_in_proj(
                h, n_tok, seq, nw, wz, wx, wdt, wdt.T, odd_conv_w[i], odd_conv_b[i].reshape(1, -1),
                odd_dt_bias[i].reshape(1, -1), odd_dt_bias[i].reshape(-1, 1))
            y = _ssd(xs, bmat, cmat, z, dt, dtt, n_tok, seq, odd_a_log[i].reshape(1, -1),
                     odd_a_log[i].reshape(-1, 1), jnp.repeat(odd_d_skip[i], SSM_HEAD_DIM).reshape(1, -1),
                     odd_norm_w[i].reshape(1, -1))
            acts, ws = [y], [odd_out_proj[i].astype(BF16)]
        wr_hi, wr_lo, br = _router_weights(router_g_w[layer], router_g_b[layer], router_e_w[layer],
                                           router_e_b[layer])
        h, route_i, route_f, counts = _mix_out_router(h, acts, ws, n_tok, norm_ffn[layer].reshape(1, d),
                                                      wr_hi, wr_lo, br)
        h = _moe_layer(h, n_tok, route_i, route_f, counts, norm_ffn[layer].reshape(1, d),
                       layer, expert_w_gate, expert_w_up, expert_w_down, fw, final_norm=(layer == depth - 1))
    return h[:n_tok * N_LANE_TILES].reshape(bsz, seq, d)
```

```python
import functools

import numpy as np
import jax
import jax.numpy as jnp
from jax import lax
from jax.experimental import pallas as pl
from jax.experimental.pallas import tpu as pltpu

F32 = jnp.float32
BF16 = jnp.bfloat16

D_MODEL = 1024
RMS_EPS = 1e-6
CONV_WIDTH = 512
SHORT_CONV_K = 3
SB_HEAD_DIM = 64
SB_WIDTH = 512
D_INNER = 2048
SSM_HEAD_DIM = 64
SSM_HEADS = 32
SSM_GROUPS = 8
SSM_HPG = 4
SSM_STATE = 128
SSM_CONV_K = 4
SSM_GN = SSM_GROUPS * SSM_STATE
SSM_CONV_DIM = D_INNER + 2 * SSM_GN
N_EXPERT_GROUPS = 4
EXPERTS_PER_GROUP = 8
EXPERT_FF = 512
N_PAIRS = EXPERTS_PER_GROUP * (EXPERTS_PER_GROUP - 1) // 2
N_BUCKETS = N_EXPERT_GROUPS * N_PAIRS

LANES = 128
SUBLANES = 8
N_LANE_TILES = D_MODEL // LANES
VMEM_LIMIT = 56 * 1024 * 1024

ROW_TILE = 512
SB_BLOCK = 256
SB_GROUP = 8
SSD_CHUNK = 128
MOE_BLOCK = 320


def _params(*sem):
    return pltpu.CompilerParams(dimension_semantics=sem, vmem_limit_bytes=VMEM_LIMIT)


def _rms(x, w):
    return x * lax.rsqrt(jnp.mean(x * x, axis=-1, keepdims=True) + RMS_EPS) * w


def _dot(a, b):
    return jnp.dot(a, b, preferred_element_type=F32)


def _load_rows(ref):
    if ref.shape[1] == D_MODEL:
        return ref[...]
    n = ref.shape[0] // N_LANE_TILES
    return jnp.concatenate([ref[pl.ds(s, n, stride=N_LANE_TILES), :] for s in range(N_LANE_TILES)], axis=1)


def _store_rows(ref, val):
    if ref.shape[1] == D_MODEL:
        ref[...] = val
    else:
        n = ref.shape[0] // N_LANE_TILES
        for s in range(N_LANE_TILES):
            ref[pl.ds(s, n, stride=N_LANE_TILES), :] = val[:, s * LANES:(s + 1) * LANES]


def _rows_spec(h, tm):
    if h.shape[1] == D_MODEL:
        return pl.BlockSpec((tm, D_MODEL), lambda i: (i, 0))
    return pl.BlockSpec((tm * N_LANE_TILES, LANES), lambda i: (i, 0))


def _dot_nt(a, b):
    return lax.dot_general(a, b, (((1,), (1,)), ((), ())), preferred_element_type=F32)


def _even_in_kernel(h_ref, nw_ref, w_ref, cw_ref, conv_ref, q_ref, k_ref, v_ref, ubuf,
                    *, tm, tiles_per_seq):
    i = pl.program_id(0)
    xb = _rms(_load_rows(h_ref), nw_ref[...]).astype(BF16)
    c = CONV_WIDTH

    def proj(c0):
        return _dot(xb, w_ref[:, c0:c0 + c])

    gate_b = proj(0)
    u = proj(c) * proj(2 * c)

    @pl.when(i % tiles_per_seq == 0)
    def _():
        ubuf[0:SUBLANES, :] = jnp.zeros((SUBLANES, c), F32)

    ubuf[SUBLANES:SUBLANES + tm, :] = u
    cw = cw_ref[...]
    conv = cw[2:3, :] * u + cw[1:2, :] * ubuf[SUBLANES - 1:SUBLANES - 1 + tm, :] \
        + cw[0:1, :] * ubuf[SUBLANES - 2:SUBLANES - 2 + tm, :]
    conv_ref[...] = (gate_b * conv).astype(BF16)
    ubuf[0:SUBLANES, :] = ubuf[tm:tm + SUBLANES, :]
    q_ref[...] = (proj(3 * c) * (SB_HEAD_DIM ** -0.5)).astype(BF16)
    k_ref[...] = proj(4 * c).astype(BF16)
    v_ref[...] = proj(5 * c).astype(BF16)


def _even_in_proj(h, n_tok, seq, nw, w_in, conv_w):
    tm = ROW_TILE
    c = CONV_WIDTH
    row = lambda i: (i, 0)
    fixed = lambda i: (0, 0)
    out = jax.ShapeDtypeStruct((n_tok, c), BF16)
    return pl.pallas_call(
        functools.partial(_even_in_kernel, tm=tm, tiles_per_seq=seq // tm),
        grid=(n_tok // tm,),
        in_specs=[_rows_spec(h, tm), pl.BlockSpec((1, D_MODEL), fixed),
                  pl.BlockSpec((D_MODEL, 6 * c), fixed), pl.BlockSpec((SHORT_CONV_K, c), fixed)],
        out_specs=[pl.BlockSpec((tm, c), row)] * 4,
        out_shape=[out] * 4,
        scratch_shapes=[pltpu.VMEM((tm + SUBLANES, c), F32)],
        compiler_params=_params("arbitrary"),
        name="even_in_proj",
    )(h, nw, w_in, conv_w)


def _sb_kernel(q_ref, k_ref, v_ref, tri_ref, o_ref, *, seq, bq, group):
    nq = seq // bq
    lane = lax.broadcasted_iota(jnp.int32, (bq, LANES), 1)
    rr = lax.broadcasted_iota(jnp.int32, (bq, bq), 0)
    cc = lax.broadcasted_iota(jnp.int32, (bq, bq), 1)
    causal = cc < rr
    every = cc >= 0
    tri = tri_ref[...]

    def block(qs, kb, vb, run, acc, mask):
        z = _dot_nt(qs, kb)
        sp = jnp.maximum(z, 0.0) + jnp.log(1.0 + jnp.exp(-jnp.abs(z)))
        spm = sp if mask is None else jnp.where(mask, sp, 0.0)
        suffix = _dot(spm.astype(BF16), tri) + run
        p = jnp.exp((z - sp) + suffix)
        if mask is not None:
            p = jnp.where(mask, p, 0.0)
        acc = acc + _dot(p.astype(BF16), vb)
        run = run - jnp.sum(spm, axis=1, keepdims=True)
        return run, acc

    def kv(kj):
        rows = pl.ds(pl.multiple_of(kj * bq, bq), bq)
        return k_ref[rows, :], v_ref[rows, :]

    def heads(qi):
        q = q_ref[qi * bq:(qi + 1) * bq, :]
        return [jnp.where((lane < SB_HEAD_DIM) == (head == 0), q, jnp.zeros_like(q)) for head in range(2)]

    def store(qi, acc):
        o_ref[qi * bq:(qi + 1) * bq, :] = jnp.where(lane < SB_HEAD_DIM, acc[:bq], acc[bq:]).astype(BF16)

    for q0 in range(0, nq, group):
        qhs = [heads(q0 + j) for j in range(group)]
        run = acc = None
        for kk in range(group - 1, -1, -1):
            kb, vb = kv(q0 + kk)
            qs = jnp.concatenate([q for j in range(kk, group) for q in qhs[j]], axis=0)
            mask = jnp.concatenate([causal, causal] + [every] * (2 * (group - 1 - kk)), axis=0)
            run0 = jnp.zeros((2 * bq, 1), F32)
            acc0 = jnp.zeros((2 * bq, LANES), F32)
            run = run0 if run is None else jnp.concatenate([run0, run], axis=0)
            acc = acc0 if acc is None else jnp.concatenate([acc0, acc], axis=0)
            run, acc = block(qs, kb, vb, run, acc, mask)
        if q0 > 0:
            def body(it, cr, qs=qs, q0=q0):
                kb, vb = kv(q0 - 1 - it)
                return block(qs, kb, vb, cr[0], cr[1], None)
            run, acc = lax.fori_loop(0, q0, body, (run, acc))
        for j in range(group):
            store(q0 + j, acc[2 * j * bq:2 * (j + 1) * bq])


def _stick_breaking(q, k, v, n_tok, seq):
    bq = SB_BLOCK
    tri = jnp.asarray(-np.tril(np.ones((bq, bq), np.float32), -1), BF16)
    blk = pl.BlockSpec((seq, LANES), lambda b, hp: (b, hp))
    return pl.pallas_call(
        functools.partial(_sb_kernel, seq=seq, bq=bq, group=SB_GROUP),
        grid=(n_tok // seq, SB_WIDTH // LANES),
        in_specs=[blk, blk, blk, pl.BlockSpec((bq, bq), lambda b, hp: (0, 0))],
        out_specs=blk,
        out_shape=jax.ShapeDtypeStruct((n_tok, SB_WIDTH), BF16),
        compiler_params=_params("arbitrary", "arbitrary"),
        name="stick_breaking",
    )(q, k, v, tri)


def _mix_out_kernel(*refs, n_act, tm):
    h_ref = refs[0]
    act_refs = refs[1:1 + n_act]
    w_refs = refs[1 + n_act:1 + 2 * n_act]
    nw_ref, wrh_ref, wrl_ref, br_ref, triu_ref = refs[1 + 2 * n_act:6 + 2 * n_act]
    hout_ref, ri_ref, rf_ref, cnt_ref, carry = refs[6 + 2 * n_act:]
    i = pl.program_id(0)

    @pl.when(i == 0)
    def _():
        carry[...] = jnp.zeros(carry.shape, F32)

    acc = _load_rows(h_ref)
    for a_ref, w_ref in zip(act_refs, w_refs):
        acc = acc + _dot(a_ref[...], w_ref[...])
    _store_rows(hout_ref, acc)

    xn = _rms(acc, nw_ref[...])
    xh = xn.astype(BF16)
    xl = (xn - xh.astype(F32)).astype(BF16)
    wrh = wrh_ref[...]
    logits = _dot(xh, wrh) + _dot(xl, wrh) + _dot(xh, wrl_ref[...]) + br_ref[...]
    lt = logits.T
    r = lax.broadcasted_iota(jnp.int32, (LANES, tm), 0).astype(F32)
    big = float(LANES)
    neg = -jnp.inf

    def first_argmax(vals):
        m = jnp.max(vals, axis=0, keepdims=True)
        return m, jnp.min(jnp.where(vals == m, r, big), axis=0, keepdims=True)

    gl = jnp.where(r < N_EXPERT_GROUPS, lt, neg)
    gmax, gidx = first_argmax(gl)
    g_top_p = 1.0 / jnp.sum(jnp.exp(gl - gmax), axis=0, keepdims=True)
    lo = N_EXPERT_GROUPS + EXPERTS_PER_GROUP * gidx
    el = jnp.where((r >= lo) & (r < lo + EXPERTS_PER_GROUP), lt, neg)
    m1, i1 = first_argmax(el)
    m2, i2 = first_argmax(jnp.where(r == i1, neg, el))
    t = jnp.exp(m2 - m1)
    g1 = g_top_p / (1.0 + t)
    g2 = g_top_p * t / (1.0 + t)
    e1 = i1 - lo
    e2 = i2 - lo
    ea = jnp.minimum(e1, e2)
    eb = jnp.maximum(e1, e2)
    pair = ea * (2 * EXPERTS_PER_GROUP - 1 - ea) * 0.5 + (eb - ea - 1.0)
    bucket = gidx * N_PAIRS + pair
    first_is_a = e1 < e2
    gate_a = jnp.where(first_is_a, g1, g2)
    gate_b = jnp.where(first_is_a, g2, g1)

    onehot = r == bucket
    cum = _dot(jnp.where(onehot, 1.0, 0.0).astype(BF16), triu_ref[...])
    prev = carry[...]
    rank = jnp.sum(jnp.where(onehot, cum + prev[:, 0:1], 0.0), axis=0, keepdims=True)
    new = prev + jnp.sum(jnp.where(onehot, 1.0, 0.0), axis=1, keepdims=True)
    carry[...] = new
    cnt_ref[...] = new

    row8 = lax.broadcasted_iota(jnp.int32, (SUBLANES, tm), 0)
    ri_ref[...] = jnp.where(row8 == 0, bucket, jnp.where(row8 == 1, rank, 0.0)).astype(jnp.int32)
    rf_ref[...] = jnp.where(row8 == 0, gate_a, jnp.where(row8 == 1, gate_b, 0.0))


def _mix_out_router(h, acts, ws, n_tok, nw, wr_hi, wr_lo, br):
    tm = ROW_TILE
    row = lambda i: (i, 0)
    col = lambda i: (0, i)
    fixed = lambda i: (0, 0)
    triu = jnp.asarray(np.triu(np.ones((tm, tm), np.float32), 1), BF16)
    in_specs = [_rows_spec(h, tm)]
    in_specs += [pl.BlockSpec((tm, a.shape[1]), row) for a in acts]
    in_specs += [pl.BlockSpec(w.shape, fixed) for w in ws]
    in_specs += [pl.BlockSpec((1, D_MODEL), fixed), pl.BlockSpec((D_MODEL, LANES), fixed),
                 pl.BlockSpec((D_MODEL, LANES), fixed), pl.BlockSpec((1, LANES), fixed),
                 pl.BlockSpec((tm, tm), fixed)]
    return pl.pallas_call(
        functools.partial(_mix_out_kernel, n_act=len(acts), tm=tm),
        grid=(n_tok // tm,),
        in_specs=in_specs,
        out_specs=[pl.BlockSpec((tm * N_LANE_TILES, LANES), row), pl.BlockSpec((SUBLANES, tm), col),
                   pl.BlockSpec((SUBLANES, tm), col), pl.BlockSpec((LANES, LANES), fixed)],
        out_shape=[jax.ShapeDtypeStruct((n_tok * N_LANE_TILES, LANES), F32),
                   jax.ShapeDtypeStruct((SUBLANES, n_tok), jnp.int32),
                   jax.ShapeDtypeStruct((SUBLANES, n_tok), F32),
                   jax.ShapeDtypeStruct((LANES, LANES), F32)],
        scratch_shapes=[pltpu.VMEM((LANES, LANES), F32)],
        compiler_params=_params("arbitrary"),
        name="mix_out_router",
    )(h, *acts, *ws, nw, wr_hi, wr_lo, br, triu)


def _ffn_kernel(nused_ref, ea_ref, eb_ref, gcur_ref, gnext_ref, sprev_ref, g_ref, nw_ref,
                wga_ref, wua_ref, wda_ref, wgb_ref, wub_ref, wdb_ref, fw_ref, h_hbm,
                out_hbm, xbuf0, xbuf1, obuf0, obuf1, gsem, ssem, *, bm, n_tok, rpo, final_norm):
    del ea_ref, eb_ref
    i = pl.program_id(0)
    slot = i % 2
    nused = nused_ref[0]
    xbufs = (xbuf0, xbuf1)
    obufs = (obuf0, obuf1)

    rpi = N_LANE_TILES

    def tok_rows(t, n):
        return pl.ds(t * n if isinstance(t, int) else pl.multiple_of(t * n, n), n)

    def row_in(gref, s, r):
        return pltpu.make_async_copy(h_hbm.at[tok_rows(gref[0, 0, r], rpi)], xbufs[s].at[tok_rows(r, rpi)],
                                     gsem.at[s])

    def row_out(s, r):
        return pltpu.make_async_copy(obufs[s].at[tok_rows(r, rpo)], out_hbm.at[tok_rows(sprev_ref[0, 0, r], rpo)],
                                     ssem.at[s])

    def wait_gather(s):
        pltpu.make_async_copy(h_hbm.at[pl.ds(0, bm * rpi)], xbufs[s], gsem.at[s]).wait()

    def wait_scatter(s):
        pltpu.make_async_copy(obufs[s], out_hbm.at[pl.ds(0, bm * rpo)], ssem.at[s]).wait()

    @pl.when(i == 0)
    def _():
        def body(r, c):
            row_in(gcur_ref, 0, r).start()
            return c
        lax.fori_loop(0, bm, body, 0, unroll=8)
        for s in range(2):
            obufs[s][...] = jnp.zeros(obufs[s].shape, F32)
            cp = pltpu.make_async_copy(obufs[s], out_hbm.at[pl.ds((n_tok + s * bm) * rpo, bm * rpo)], ssem.at[s])
            cp.start()
            if s == 1:
                cp.wait()

    def run_block(cur, nxt):
        for r in range(bm):
            row_in(gnext_ref, nxt, r).start()
        wait_gather(cur)
        x = _load_rows(xbufs[cur])
        xb = _rms(x, nw_ref[...]).astype(BF16)
        g = g_ref[...]

        def hidden(wg_ref, wu_ref, gate):
            return (jax.nn.silu(_dot(xb, wg_ref[0, 0].astype(BF16))) * _dot(xb, wu_ref[0, 0].astype(BF16))
                    * gate).astype(BF16)

        hid_a = hidden(wga_ref, wua_ref, g[:, 0:1])
        hid_b = hidden(wgb_ref, wub_ref, g[:, 1:2])
        for r in range(bm):
            row_out(nxt, r).start(priority=r % 2)
        wait_scatter(cur)
        y = x + _dot(hid_a, wda_ref[0, 0].astype(BF16)) + _dot(hid_b, wdb_ref[0, 0].astype(BF16))
        if final_norm:
            y = _rms(y, fw_ref[...])
        _store_rows(obufs[cur], y)

    for s in range(2):
        @pl.when((i < nused) & (slot == s))
        def _(s=s):
            run_block(s, 1 - s)

    @pl.when(i == nused)
    def _():
        for s in range(2):
            @pl.when(slot == s)
            def _(s=s):
                wait_gather(s)
                wait_scatter(s)

                def body(r, c):
                    row_out(1 - s, r).start()
                    return c
                lax.fori_loop(0, bm, body, 0, unroll=8)
                wait_scatter(1 - s)


def _moe_ffn(h, n_tok, nused, blk_ea, blk_eb, gather_rows, scatter_rows, gate_rows, nw, layer, wg, wu, wd, fw,
             final_norm):
    bm = MOE_BLOCK
    n_blocks = gather_rows.shape[0]
    rpo = 1 if final_norm else N_LANE_TILES
    out_cols = D_MODEL // rpo
    cur = lambda i, nu, ea, eb: (i, 0, 0)
    nxt = lambda i, nu, ea, eb: (jnp.minimum(i + 1, n_blocks - 1), 0, 0)
    idx_spec = lambda imap: pl.BlockSpec((1, 1, bm), imap, memory_space=pltpu.SMEM)
    fixed = lambda i, nu, ea, eb: (0, 0)
    wa = lambda i, nu, ea, eb: (layer, ea[i], 0, 0)
    wb = lambda i, nu, ea, eb: (layer, eb[i], 0, 0)
    up = pl.BlockSpec((1, 1, D_MODEL, EXPERT_FF), wa)
    upb = pl.BlockSpec((1, 1, D_MODEL, EXPERT_FF), wb)
    grid_spec = pltpu.PrefetchScalarGridSpec(
        num_scalar_prefetch=3,
        grid=(n_blocks,),
        in_specs=[idx_spec(cur), idx_spec(nxt), idx_spec(cur),
                  pl.BlockSpec((bm, 2), lambda i, nu, ea, eb: (i, 0)),
                  pl.BlockSpec((1, D_MODEL), fixed),
                  up, up, pl.BlockSpec((1, 1, EXPERT_FF, D_MODEL), wa),
                  upb, upb, pl.BlockSpec((1, 1, EXPERT_FF, D_MODEL), wb),
                  pl.BlockSpec((1, D_MODEL), fixed),
                  pl.BlockSpec(memory_space=pl.ANY)],
        out_specs=pl.BlockSpec(memory_space=pl.ANY),
        scratch_shapes=[pltpu.VMEM((bm * N_LANE_TILES, LANES), F32)] * 2 + [pltpu.VMEM((bm * rpo, out_cols), F32)] * 2
        + [pltpu.SemaphoreType.DMA((2,)), pltpu.SemaphoreType.DMA((2,))],
    )
    return pl.pallas_call(
        functools.partial(_ffn_kernel, bm=bm, n_tok=n_tok, rpo=rpo, final_norm=final_norm),
        grid_spec=grid_spec,
        out_shape=jax.ShapeDtypeStruct(((n_tok + 2 * bm) * rpo, out_cols), F32),
        compiler_params=_params("arbitrary"),
        name="moe_ffn",
    )(nused, blk_ea, blk_eb, gather_rows, gather_rows, scatter_rows, gate_rows, nw, wg, wu, wd, wg, wu, wd,
      fw, h)


_PAIR_A = np.array([a for a in range(EXPERTS_PER_GROUP) for b in range(a + 1, EXPERTS_PER_GROUP)], np.int32)
_PAIR_B = np.array([b for a in range(EXPERTS_PER_GROUP) for b in range(a + 1, EXPERTS_PER_GROUP)], np.int32)


def _moe_layer(h, n_tok, route_i, route_f, counts, nw, layer, wg, wu, wd, fw, final_norm):
    bm = MOE_BLOCK
    n_blocks = n_tok // bm + N_BUCKETS + 1
    bucket = route_i[0]
    rank = route_i[1]
    cnt = counts[:N_BUCKETS, 0].astype(jnp.int32)
    nblk = (cnt + bm - 1) // bm
    blk_end = jnp.cumsum(nblk)
    blk_start = blk_end - nblk
    in_bucket = bucket[:, None] == jnp.arange(N_BUCKETS, dtype=jnp.int32)[None, :]
    dest = jnp.sum(jnp.where(in_bucket, blk_start[None, :], 0), axis=1) * bm + rank
    upd = jnp.stack([jnp.arange(n_tok, dtype=jnp.int32), lax.bitcast_convert_type(route_f[0], jnp.int32),
                     lax.bitcast_convert_type(route_f[1], jnp.int32)], axis=1)
    init = jnp.broadcast_to(jnp.array([n_tok, 0, 0], jnp.int32), (n_blocks * bm, 3))
    packed = init.at[dest].set(upd, unique_indices=True)
    tok_rows = packed[:, 0]
    gate_rows = lax.bitcast_convert_type(packed[:, 1:3], F32)
    blk_bucket = jnp.minimum(
        jnp.searchsorted(blk_end, jnp.arange(n_blocks, dtype=jnp.int32), side='right'),
        N_BUCKETS - 1).astype(jnp.int32)
    nused = blk_end[-1:].astype(jnp.int32)
    last = blk_bucket[jnp.maximum(nused[0] - 1, 0)]
    blk_bucket = jnp.where(jnp.arange(n_blocks) < nused[0], blk_bucket, last)
    grp = blk_bucket // N_PAIRS
    pair = blk_bucket % N_PAIRS
    blk_ea = grp * EXPERTS_PER_GROUP + jnp.asarray(_PAIR_A)[pair]
    blk_eb = grp * EXPERTS_PER_GROUP + jnp.asarray(_PAIR_B)[pair]
    row = jnp.arange(n_blocks * bm, dtype=jnp.int32)
    gather_rows = jnp.minimum(tok_rows, n_tok - 1)
    scatter_rows = jnp.where(tok_rows >= n_tok, n_tok + (row // bm) % 2 * bm + row % bm, tok_rows)
    scatter_rows = jnp.concatenate([n_tok + bm + row[:bm], scatter_rows])
    return _moe_ffn(h, n_tok, nused, blk_ea, blk_eb, gather_rows.reshape(n_blocks, 1, bm),
                    scatter_rows.reshape(n_blocks + 1, 1, bm), gate_rows, nw, layer, wg, wu, wd, fw,
                    final_norm)


def _odd_in_kernel(h_ref, nw_ref, wz_ref, wx_ref, wdt_ref, wdtt_ref, cw_ref, cb_ref, dtb_ref, dtbt_ref,
                   z_ref, x_ref, b_ref, c_ref, dt_ref, dtt_ref, ubuf, *, tm, tiles_per_seq):
    i = pl.program_id(0)
    xb = _rms(_load_rows(h_ref), nw_ref[...]).astype(BF16)
    cw_all = cw_ref[...]
    cb_all = cb_ref[...]
    wchunk = 512

    @pl.when(i % tiles_per_seq == 0)
    def _():
        ubuf[0:SUBLANES, :] = jnp.zeros((SUBLANES, SSM_CONV_DIM), F32)

    for c0 in range(0, D_INNER, wchunk):
        z_ref[:, c0:c0 + wchunk] = _dot(xb, wz_ref[:, c0:c0 + wchunk]).astype(BF16)

    for c0 in range(0, SSM_CONV_DIM, wchunk):
        cols = slice(c0, c0 + wchunk)
        u = _dot(xb, wx_ref[:, cols])
        ubuf[SUBLANES:SUBLANES + tm, cols] = u
        cw = cw_all[:, cols]
        conv = cw[3:4, :] * u + cb_all[:, cols]
        for j in range(1, SSM_CONV_K):
            conv = conv + cw[3 - j:4 - j, :] * ubuf[SUBLANES - j:SUBLANES - j + tm, cols]
        act = jax.nn.silu(conv).astype(BF16)
        if c0 < D_INNER:
            x_ref[:, c0:c0 + wchunk] = act
        elif c0 < D_INNER + SSM_GN:
            b_ref[:, c0 - D_INNER:c0 - D_INNER + wchunk] = act
        else:
            c_ref[:, c0 - D_INNER - SSM_GN:c0 - D_INNER - SSM_GN + wchunk] = act
    ubuf[0:SUBLANES, :] = ubuf[tm:tm + SUBLANES, :]

    dt_ref[...] = jax.nn.softplus(_dot(xb, wdt_ref[...]) + dtb_ref[...])
    dtt_ref[...] = jax.nn.softplus(_dot_nt(wdtt_ref[...], xb) + dtbt_ref[...])


def _odd_in_proj(h, n_tok, seq, nw, wz, wx, wdt, wdtt, conv_w, conv_b, dtb, dtbt):
    tm = ROW_TILE
    row = lambda i: (i, 0)
    col = lambda i: (0, i)
    fixed = lambda i: (0, 0)
    full = lambda a: pl.BlockSpec(a.shape, fixed)
    return pl.pallas_call(
        functools.partial(_odd_in_kernel, tm=tm, tiles_per_seq=seq // tm),
        grid=(n_tok // tm,),
        in_specs=[_rows_spec(h, tm), full(nw), full(wz), full(wx), full(wdt), full(wdtt),
                  full(conv_w), full(conv_b), full(dtb), full(dtbt)],
        out_specs=[pl.BlockSpec((tm, D_INNER), row), pl.BlockSpec((tm, D_INNER), row),
                   pl.BlockSpec((tm, SSM_GN), row), pl.BlockSpec((tm, SSM_GN), row),
                   pl.BlockSpec((tm, SSM_HEADS), row), pl.BlockSpec((SSM_HEADS, tm), col)],
        out_shape=[jax.ShapeDtypeStruct((n_tok, D_INNER), BF16), jax.ShapeDtypeStruct((n_tok, D_INNER), BF16),
                   jax.ShapeDtypeStruct((n_tok, SSM_GN), BF16), jax.ShapeDtypeStruct((n_tok, SSM_GN), BF16),
                   jax.ShapeDtypeStruct((n_tok, SSM_HEADS), F32), jax.ShapeDtypeStruct((SSM_HEADS, n_tok), F32)],
        scratch_shapes=[pltpu.VMEM((tm + SUBLANES, SSM_CONV_DIM), F32)],
        compiler_params=_params("arbitrary"),
        name="odd_in_proj",
    )(h, nw, wz, wx, wdt, wdtt, conv_w, conv_b, dtb, dtbt)


def _ssd_kernel(x_ref, b_ref, c_ref, z_ref, dt_ref, dtt_ref, alog_ref, alogt_ref, dskip_ref, normw_ref,
                tril_ref, triu_ref, spread_ref, y_ref, state, *, lc):
    ci = pl.program_id(1)

    @pl.when(ci == 0)
    def _():
        state[...] = jnp.zeros(state.shape, F32)

    gw = SSM_HPG * SSM_HEAD_DIM
    hi = lax.Precision.HIGHEST
    dt = dt_ref[...]
    dtt = dtt_ref[...]
    a = -jnp.exp(alog_ref[...])
    at = -jnp.exp(alogt_ref[...])
    cs = jnp.dot(tril_ref[...], dt * a, precision=hi, preferred_element_type=F32)
    cst = jnp.dot(dtt * at, triu_ref[...], precision=hi, preferred_element_type=F32)
    cs_last = cs[lc - 1:lc, :]
    grow = jnp.exp(cs)
    dt_end = dt * jnp.exp(cs_last - cs)
    e_last = jnp.exp(cs_last)
    rr = lax.broadcasted_iota(jnp.int32, (lc, lc), 0)
    cc = lax.broadcasted_iota(jnp.int32, (lc, lc), 1)
    lower = cc <= rr
    lane = lax.broadcasted_iota(jnp.int32, (lc, gw), 1)
    head_of_lane = lane // SSM_HEAD_DIM
    fac = jnp.concatenate([dt, grow, dt_end, jnp.broadcast_to(e_last, (SUBLANES, SSM_HEADS))], axis=0)
    fac_hi = fac.astype(BF16).astype(F32)
    spread = _dot(jnp.concatenate([fac_hi, fac - fac_hi], axis=1).astype(BF16), spread_ref[...])

    for g in range(SSM_GROUPS):
        h0 = g * SSM_HPG
        bg = b_ref[:, g * SSM_STATE:(g + 1) * SSM_STATE]
        cg = c_ref[:, g * SSM_STATE:(g + 1) * SSM_STATE]
        xg = x_ref[:, g * gw:(g + 1) * gw].astype(F32)
        cb = _dot_nt(cg, bg)
        cols = slice(g * gw, (g + 1) * gw)
        xdt = xg * spread[0:lc, cols]
        xdt_b = xdt.astype(BF16)
        st = state[g]
        y = _dot(cg, st.astype(BF16)) * spread[lc:2 * lc, cols]
        for k in range(SSM_HPG):
            h = h0 + k
            seg = cs[:, h:h + 1] - cst[h:h + 1, :]
            decay = jnp.exp(jnp.where(lower, seg, -jnp.inf))
            m = (cb * decay).astype(BF16)
            y = y + _dot(m, jnp.where(head_of_lane == k, xdt_b, jnp.zeros_like(xdt_b)))
        x_end = (xg * spread[2 * lc:3 * lc, cols]).astype(BF16)
        bgt = bg.astype(F32).T.astype(BF16)
        state[g] = st * spread[3 * lc:3 * lc + 1, cols] + _dot(bgt, x_end)
        y = y + dskip_ref[:, g * gw:(g + 1) * gw] * xg
        y = y * jax.nn.silu(z_ref[:, g * gw:(g + 1) * gw].astype(F32))
        y = y * lax.rsqrt(jnp.mean(y * y, axis=-1, keepdims=True) + RMS_EPS)
        y_ref[:, g * gw:(g + 1) * gw] = (y * normw_ref[:, g * gw:(g + 1) * gw]).astype(BF16)


def _ssd(x, bm, cm, z, dt, dtt, n_tok, seq, alog, alogt, dskip, normw):
    lc = SSD_CHUNK
    nc = seq // lc
    tril = jnp.asarray(np.tril(np.ones((lc, lc), np.float32)))
    triu = jnp.asarray(np.triu(np.ones((lc, lc), np.float32)))
    head_of_col = np.arange(D_INNER) // SSM_HEAD_DIM
    spread = jnp.asarray(np.arange(2 * SSM_HEADS)[:, None] % SSM_HEADS == head_of_col[None, :], BF16)
    row = lambda b, c: (b * nc + c, 0)
    col = lambda b, c: (0, b * nc + c)
    fixed = lambda b, c: (0, 0)
    full = lambda a: pl.BlockSpec(a.shape, fixed)
    return pl.pallas_call(
        functools.partial(_ssd_kernel, lc=lc),
        grid=(n_tok // seq, nc),
        in_specs=[pl.BlockSpec((lc, D_INNER), row), pl.BlockSpec((lc, SSM_GN), row),
                  pl.BlockSpec((lc, SSM_GN), row), pl.BlockSpec((lc, D_INNER), row),
                  pl.BlockSpec((lc, SSM_HEADS), row), pl.BlockSpec((SSM_HEADS, lc), col),
                  full(alog), full(alogt), full(dskip), full(normw), full(tril), full(triu),
                  full(spread)],
        out_specs=pl.BlockSpec((lc, D_INNER), row),
        out_shape=jax.ShapeDtypeStruct((n_tok, D_INNER), BF16),
        scratch_shapes=[pltpu.VMEM((SSM_GROUPS, SSM_STATE, SSM_HPG * SSM_HEAD_DIM), F32)],
        compiler_params=_params("arbitrary", "arbitrary"),
        name="ssd_scan",
    )(x, bm, cm, z, dt, dtt, alog, alogt, dskip, normw, tril, triu, spread)


def _router_weights(wg, bg, we, be):
    w = jnp.concatenate([wg, jnp.moveaxis(we, 0, 1).reshape(D_MODEL, -1)], axis=1)
    b = jnp.concatenate([bg, be.reshape(-1)])
    pad = LANES - w.shape[1]
    w = jnp.pad(w, ((0, 0), (0, pad)))
    b = jnp.pad(b, (0, pad)).reshape(1, LANES)
    w_hi = w.astype(BF16)
    w_lo = (w - w_hi.astype(F32)).astype(BF16)
    return w_hi, w_lo, b


def kernel(x, even_w_in, even_conv_w, even_w_out, odd_in_proj, odd_conv_w, odd_conv_b, odd_dt_bias,
           odd_a_log, odd_d_skip, odd_norm_w, odd_out_proj, norm_mix, norm_ffn, router_g_w, router_g_b,
           router_e_w, router_e_b, expert_w_gate, expert_w_up, expert_w_down, final_norm):
    bsz, seq, d = x.shape
    n_tok = bsz * seq
    depth = norm_mix.shape[0]
    h = x.reshape(n_tok, d)
    fw = final_norm.reshape(1, d)
    for layer in range(depth):
        i = layer // 2
        nw = norm_mix[layer].reshape(1, d)
        if layer % 2 == 0:
            conv, q, k, v = _even_in_proj(h, n_tok, seq, nw, even_w_in[i].astype(BF16), even_conv_w[i])
            sb = _stick_breaking(q, k, v, n_tok, seq)
            w_out = even_w_out[i].astype(BF16)
            acts, ws = [conv, sb], [w_out[:CONV_WIDTH], w_out[CONV_WIDTH:]]
        else:
            w_in = odd_in_proj[i]
            wz = w_in[:, :D_INNER].astype(BF16)
            wx = w_in[:, D_INNER:D_INNER + SSM_CONV_DIM].astype(BF16)
            wdt = w_in[:, D_INNER + SSM_CONV_DIM:].astype(BF16)
            z, xs, bmat, cmat, dt, dtt = _odd_in_proj(
                h, n_tok, seq, nw, wz, wx, wdt, wdt.T, odd_conv_w[i], odd_conv_b[i].reshape(1, -1),
                odd_dt_bias[i].reshape(1, -1), odd_dt_bias[i].reshape(-1, 1))
            y = _ssd(xs, bmat, cmat, z, dt, dtt, n_tok, seq, odd_a_log[i].reshape(1, -1),
                     odd_a_log[i].reshape(-1, 1), jnp.repeat(odd_d_skip[i], SSM_HEAD_DIM).reshape(1, -1),
                     odd_norm_w[i].reshape(1, -1))
            acts, ws = [y], [odd_out_proj[i].astype(BF16)]
        wr_hi, wr_lo, br = _router_weights(router_g_w[layer], router_g_b[layer], router_e_w[layer],
                                           router_e_b[layer])
        h, route_i, route_f, counts = _mix_out_router(h, acts, ws, n_tok, norm_ffn[layer].reshape(1, d),
                                                      wr_hi, wr_lo, br)
        h = _moe_layer(h, n_tok, route_i, route_f, counts, norm_ffn[layer].reshape(1, d),
                       layer, expert_w_gate, expert_w_up, expert_w_down, fw, final_norm=(layer == depth - 1))
    return h[:n_tok].reshape(bsz, seq, d)
```
